```python
import jax, jax.numpy as jnp
from jax import lax
import numpy as np

D_MODEL = 1024
BATCH = 8
SEQ = 4096
DEPTH = 1
DEC_BATCH = 128
DEC_SEQ = 1
PAST_LEN = 8192
PAGE_SIZE = 128

N_HEADS = 8
HEAD_DIM = 64
ATTN_WIDTH = N_HEADS * HEAD_DIM
MOBA_BLOCK = 256
MOBA_TOP_K = 3
POOL_WINDOWS = (2, 4, 8, 16)
POOL_GROUPS = len(POOL_WINDOWS)
POOL_GROUP_WIDTH = 128
POOL_WIDTH = POOL_GROUPS * POOL_GROUP_WIDTH
POOL_HIST = max(POOL_WINDOWS) - 1
D_FF = 2816
IN_WIDTH = POOL_WIDTH + 3 * ATTN_WIDTH + 2 * D_MODEL
Q_CHUNK = 16
EPS = 1e-6
NEG = -1e30
SCALE = HEAD_DIM ** -0.5

kernel_name = "moba_pool_gated_hybrid_step"


def _rmsnorm(x, g):
    xf = x.astype(jnp.float32)
    y = xf * lax.rsqrt(jnp.mean(xf * xf, axis=-1, keepdims=True) + EPS)
    return (y * g.astype(jnp.float32)).astype(x.dtype)


def _swiglu(h, w1, w3, w2):
    return (jax.nn.silu(h @ w1) * (h @ w3)) @ w2


def _mixer_inputs(x, g_ffn1, w1_ffn1, w3_ffn1, w2_ffn1, g_mix, w_in, b_gate, g_q, g_k):
    x = x + 0.5 * _swiglu(_rmsnorm(x, g_ffn1), w1_ffn1, w3_ffn1, w2_ffn1)
    z = _rmsnorm(x, g_mix) @ w_in
    b, L, _ = z.shape
    o = POOL_WIDTH
    u = z[..., :o]
    q = _rmsnorm(z[..., o:o + ATTN_WIDTH].reshape(b, L, N_HEADS, HEAD_DIM), g_q)
    k = _rmsnorm(z[..., o + ATTN_WIDTH:o + 2 * ATTN_WIDTH].reshape(b, L, N_HEADS, HEAD_DIM), g_k)
    v = z[..., o + 2 * ATTN_WIDTH:o + 3 * ATTN_WIDTH].reshape(b, L, N_HEADS, HEAD_DIM)
    gates = jax.nn.sigmoid(z[..., o + 3 * ATTN_WIDTH:] + b_gate)
    return x, u, q, k, v, gates[..., :D_MODEL], gates[..., D_MODEL:]


def _pool_mix(u, offset, w_pool_grp, s_pool):
    uf = u.astype(jnp.float32)
    L = uf.shape[1]
    c0 = jnp.pad(jnp.cumsum(uf, axis=1), ((0, 0), (1, 0), (0, 0)))
    t = jnp.arange(L)
    outs = []
    for g, w in enumerate(POOL_WINDOWS):
        sl = slice(g * POOL_GROUP_WIDTH, (g + 1) * POOL_GROUP_WIDTH)
        cg = c0[..., sl]
        lagged = jnp.pad(cg, ((0, 0), (w, 0), (0, 0)))[:, 1:L + 1]
        cnt = jnp.minimum(w, offset + t + 1).astype(jnp.float32)
        d = (cg[:, 1:] - lagged) / cnt[None, :, None] - uf[..., sl]
        outs.append(d.astype(u.dtype) @ w_pool_grp[g])
    return jnp.concatenate(outs, axis=-1) * s_pool


def _blocks(k_all, v_all):
    b, L, h, d = k_all.shape
    nb = -(-L // MOBA_BLOCK)
    pad = ((0, 0), (0, nb * MOBA_BLOCK - L), (0, 0), (0, 0))
    kb = jnp.pad(k_all, pad).reshape(b, nb, MOBA_BLOCK, h, d)
    vb = jnp.pad(v_all, pad).reshape(b, nb, MOBA_BLOCK, h, d)
    kmean = jnp.mean(kb.astype(jnp.float32), axis=2)
    return kb, vb, kmean, min(MOBA_TOP_K, nb - 1)


def _moba_attend(q, qpos, kb, vb, kmean, n_sel):
    b, nq = q.shape[0], q.shape[1]
    nb = kb.shape[1]
    own = qpos // MOBA_BLOCK
    own_idx = jnp.broadcast_to(own[None, None, :, None], (b, N_HEADS, nq, 1))
    if n_sel > 0:
        s_blk = jnp.einsum('bqhd,bnhd->bhqn', q.astype(jnp.float32), kmean)
        past = jnp.arange(nb)[None, :] < own[:, None]
        s_blk = jnp.where(past[None, None], s_blk, NEG)
        _, sel = lax.top_k(s_blk, n_sel)
        idx = jnp.concatenate([sel, own_idx], axis=-1)
        ok = jnp.concatenate([sel < own[None, None, :, None],
                              jnp.ones_like(own_idx, dtype=bool)], axis=-1)
    else:
        idx = own_idx
        ok = jnp.ones_like(own_idx, dtype=bool)
    bi = jnp.arange(b)[:, None, None, None]
    hi = jnp.arange(N_HEADS)[None, :, None, None]
    kg = kb[bi, idx, :, hi, :]
    vg = vb[bi, idx, :, hi, :]
    kpos = idx[..., None] * MOBA_BLOCK + jnp.arange(MOBA_BLOCK)
    mask = ok[..., None] & (kpos <= qpos[None, None, :, None, None])
    s = jnp.einsum('bqhd,bhqnkd->bhqnk', q, kg, preferred_element_type=jnp.float32) * SCALE
    s = jnp.where(mask, s, NEG)
    p = jax.nn.softmax(s.reshape(b, N_HEADS, nq, -1), axis=-1).reshape(s.shape)
    o = jnp.einsum('bhqnk,bhqnkd->bqhd', p.astype(vg.dtype), vg, preferred_element_type=jnp.float32)
    return o.astype(q.dtype)


def _post(x, pool_out, attn_out, gate_pool, gate_attn, w_branch_pool, w_branch_attn, w_out,
          g_ffn2, w1_ffn2, w3_ffn2, w2_ffn2):
    b, L = x.shape[0], x.shape[1]
    m = (gate_pool * (pool_out @ w_branch_pool)
         + gate_attn * (attn_out.reshape(b, L, ATTN_WIDTH) @ w_branch_attn))
    x = x + m @ w_out
    return x + 0.5 * _swiglu(_rmsnorm(x, g_ffn2), w1_ffn2, w3_ffn2, w2_ffn2)


def _layer(l, xp, xs, cache_k, cache_v, hist_pool, page_table,
           g_ffn1, w1_ffn1, w3_ffn1, w2_ffn1, g_mix, w_in, b_gate, g_q, g_k,
           w_pool_grp, s_pool, w_branch_pool, w_branch_attn, w_out,
           g_ffn2, w1_ffn2, w3_ffn2, w2_ffn2):
    pre = (g_ffn1, w1_ffn1, w3_ffn1, w2_ffn1, g_mix, w_in, b_gate, g_q, g_k)
    post = (w_branch_pool, w_branch_attn, w_out, g_ffn2, w1_ffn2, w3_ffn2, w2_ffn2)

    xp, u, q, k, v, gp, ga = _mixer_inputs(xp, *pre)
    b, seq = q.shape[0], q.shape[1]
    pool_p = _pool_mix(u, 0, w_pool_grp, s_pool)
    kb, vb, kmean, n_sel = _blocks(k, v)
    nc = seq // Q_CHUNK
    qc = q.reshape(b, nc, Q_CHUNK, N_HEADS, HEAD_DIM).swapaxes(0, 1)
    pc = jnp.arange(seq).reshape(nc, Q_CHUNK)
    attn = lax.map(lambda a: _moba_attend(a[0], a[1], kb, vb, kmean, n_sel), (qc, pc))
    attn = attn.swapaxes(0, 1).reshape(b, seq, N_HEADS, HEAD_DIM)
    yp = _post(xp, pool_p, attn, gp, ga, *post)

    xs, us, qs, ks, vs, gps, gas = _mixer_inputs(xs, *pre)
    dec_b, dec_s = qs.shape[0], qs.shape[1]
    past = page_table.shape[1] * cache_k.shape[2]
    u_ext = jnp.concatenate([hist_pool.astype(us.dtype), us], axis=1)
    pool_s = _pool_mix(u_ext, past - POOL_HIST, w_pool_grp, s_pool)[:, POOL_HIST:]
    k_past = cache_k[l, page_table].reshape(dec_b, past, N_HEADS, HEAD_DIM)
    v_past = cache_v[l, page_table].reshape(dec_b, past, N_HEADS, HEAD_DIM)
    kb_s, vb_s, kmean_s, n_sel_s = _blocks(
        jnp.concatenate([k_past, ks.astype(k_past.dtype)], axis=1),
        jnp.concatenate([v_past, vs.astype(v_past.dtype)], axis=1))
    attn_s = _moba_attend(qs, past + jnp.arange(dec_s), kb_s, vb_s, kmean_s, n_sel_s)
    ys = _post(xs, pool_s, attn_s, gps, gas, *post)

    return yp, ys, k, v, u[:, -POOL_HIST:], ks, vs, u_ext[:, -POOL_HIST:]


def _normal(key, shape, scale):
    return jax.random.normal(key, shape, jnp.float32) * scale


def setup_inputs(seed: int = 0) -> dict:
    key = jax.random.key(seed)
    ks = jax.random.split(key, 24)
    n_pages = PAST_LEN // PAGE_SIZE
    n_used = DEC_BATCH * n_pages
    n_pool = n_used + n_used // 4
    L = DEPTH
    page_table = jax.random.permutation(ks[5], n_pool)[:n_used].reshape(DEC_BATCH, n_pages).astype(jnp.int32)
    return {
        "x_prompt": _normal(ks[0], (BATCH, SEQ, D_MODEL), 1.0),
        "x_sample": _normal(ks[1], (DEC_BATCH, DEC_SEQ, D_MODEL), 1.0),
        "cache_k": _normal(ks[2], (L, n_pool, PAGE_SIZE, N_HEADS, HEAD_DIM), 1.0),
        "cache_v": _normal(ks[3], (L, n_pool, PAGE_SIZE, N_HEADS, HEAD_DIM), 1.0),
        "state_pool": _normal(ks[4], (L, DEC_BATCH, POOL_HIST, POOL_WIDTH), 1.0),
        "page_table": page_table,
        "g_ffn1": 1.0 + _normal(ks[6], (L, D_MODEL), 0.05),
        "w1_ffn1": _normal(ks[7], (L, D_MODEL, D_FF), D_MODEL ** -0.5),
        "w3_ffn1": _normal(ks[8], (L, D_MODEL, D_FF), D_MODEL ** -0.5),
        "w2_ffn1": _normal(ks[9], (L, D_FF, D_MODEL), D_FF ** -0.5),
        "g_mix": 1.0 + _normal(ks[10], (L, D_MODEL), 0.05),
        "w_in": _normal(ks[11], (L, D_MODEL, IN_WIDTH), D_MODEL ** -0.5),
        "b_gate": _normal(ks[12], (L, 2 * D_MODEL), 0.1),
        "g_q": 1.0 + _normal(ks[13], (L, HEAD_DIM), 0.05),
        "g_k": 1.0 + _normal(ks[14], (L, HEAD_DIM), 0.05),
        "w_pool_grp": _normal(ks[15], (L, POOL_GROUPS, POOL_GROUP_WIDTH, POOL_GROUP_WIDTH), POOL_GROUP_WIDTH ** -0.5),
        "s_pool": 1.0 + _normal(ks[16], (L, POOL_WIDTH), 0.1),
        "w_branch_pool": _normal(ks[17], (L, POOL_WIDTH, D_MODEL), POOL_WIDTH ** -0.5),
        "w_branch_attn": _normal(ks[18], (L, ATTN_WIDTH, D_MODEL), ATTN_WIDTH ** -0.5),
        "w_out": _normal(ks[19], (L, D_MODEL, D_MODEL), D_MODEL ** -0.5),
        "g_ffn2": 1.0 + _normal(ks[20], (L, D_MODEL), 0.05),
        "w1_ffn2": _normal(ks[21], (L, D_MODEL, D_FF), D_MODEL ** -0.5),
        "w3_ffn2": _normal(ks[22], (L, D_MODEL, D_FF), D_MODEL ** -0.5),
        "w2_ffn2": _normal(ks[23], (L, D_FF, D_MODEL), D_FF ** -0.5),
    }


def reference(x_prompt, x_sample, cache_k, cache_v, state_pool, page_table,
              g_ffn1, w1_ffn1, w3_ffn1, w2_ffn1, g_mix, w_in, b_gate, g_q, g_k,
              w_pool_grp, s_pool, w_branch_pool, w_branch_attn, w_out,
              g_ffn2, w1_ffn2, w3_ffn2, w2_ffn2):
    xp, xs = x_prompt, x_sample
    per_layer = []
    for l in range(DEPTH):
        xp, xs, *new = _layer(
            l, xp, xs, cache_k, cache_v, state_pool[l], page_table,
            g_ffn1[l], w1_ffn1[l], w3_ffn1[l], w2_ffn1[l], g_mix[l], w_in[l], b_gate[l],
            g_q[l], g_k[l], w_pool_grp[l], s_pool[l], w_branch_pool[l], w_branch_attn[l],
            w_out[l], g_ffn2[l], w1_ffn2[l], w3_ffn2[l], w2_ffn2[l])
        per_layer.append(new)
    k_p, v_p, pool_p, k_s, v_s, pool_s = [jnp.stack(c) for c in zip(*per_layer)]
    return (xp, xs, k_p, v_p, pool_p, k_s, v_s, pool_s)
```

```python
import functools

import jax
import jax.numpy as jnp
from jax import lax
from jax.experimental import pallas as pl
from jax.experimental.pallas import tpu as pltpu

N_HEADS = 8
HEAD_DIM = 64
ATTN_WIDTH = N_HEADS * HEAD_DIM
MOBA_BLOCK = 256
MOBA_TOP_K = 3
POOL_WINDOWS = (2, 4, 8, 16)
POOL_GROUP_WIDTH = 128
POOL_WIDTH = len(POOL_WINDOWS) * POOL_GROUP_WIDTH
POOL_HIST = max(POOL_WINDOWS) - 1
HIST_ROWS = 16
EPS = 1e-6
NEG = -1e30
SCALE = HEAD_DIM ** -0.5

LANES = 128
TOKEN_TILE = 512
VMEM_LIMIT = 56 * 1024 * 1024

_MXU = jnp.bfloat16
_F32 = jnp.float32


def _dot(a, b):
    return jnp.dot(a, b, preferred_element_type=_F32)


def _dot_nt(a, b):
    return lax.dot_general(a, b, (((1,), (1,)), ((), ())), preferred_element_type=_F32)


def _split(a):
    hi = a.astype(_MXU)
    lo = (a - hi.astype(_F32)).astype(_MXU)
    return hi, lo


def _rms(x, g):
    ms = jnp.mean(x * x, axis=-1, keepdims=True)
    return x * lax.rsqrt(ms + EPS) * g


def _params(*sem):
    return pltpu.CompilerParams(dimension_semantics=sem, vmem_limit_bytes=VMEM_LIMIT)


def _resident(shape):
    return pl.BlockSpec(shape, lambda *_: (0,) * len(shape), pipeline_mode=pl.Buffered(1))


def _token_tile(n):
    return TOKEN_TILE if n % TOKEN_TILE == 0 else n


def _ffn_kernel(x_ref, g_ref, w1_ref, w3_ref, w2_ref, o_ref):
    x = x_ref[...]
    h = _rms(x, g_ref[...]).astype(_MXU)
    a = _dot(h, w1_ref[...])
    b = _dot(h, w3_ref[...])
    act = (a * jax.nn.sigmoid(a) * b).astype(_MXU)
    o_ref[...] = x + 0.5 * _dot(act, w2_ref[...])


def _ffn(x, g, w1, w3, w2):
    n, d = x.shape
    f = w1.shape[1]
    tm = _token_tile(n)
    return pl.pallas_call(
        _ffn_kernel,
        grid=(n // tm,),
        in_specs=[
            pl.BlockSpec((tm, d), lambda i: (i, 0)),
            _resident((1, d)),
            _resident((d, f)),
            _resident((d, f)),
            _resident((f, d)),
        ],
        out_specs=pl.BlockSpec((tm, d), lambda i: (i, 0)),
        out_shape=jax.ShapeDtypeStruct((n, d), _F32),
        compiler_params=_params("parallel"),
        name="ffn",
    )(x, g, w1, w3, w2)


def _head_rms(z, g, pmat):
    z2 = z * z
    half = pmat.shape[0]
    ms = []
    for c in range(z.shape[1] // half):
        hi, lo = _split(z2[:, c * half:(c + 1) * half])
        ms.append(_dot(hi, pmat) + _dot(lo, pmat))
    ms = jnp.concatenate(ms, axis=1)
    return z * lax.rsqrt(ms + EPS) * g


def _proj_kernel(x_ref, g_ref, w_ref, b_ref, gq_ref, gk_ref, p_ref,
                 u_ref, q_ref, k_ref, v_ref, gt_ref):
    h = _rms(x_ref[...], g_ref[...]).astype(_MXU)
    a = ATTN_WIDTH
    o = POOL_WIDTH
    u_ref[...] = _dot(h, w_ref[:, 0:o])
    q_ref[...] = _head_rms(_dot(h, w_ref[:, o:o + a]), gq_ref[...], p_ref[...])
    k_ref[...] = _head_rms(_dot(h, w_ref[:, o + a:o + 2 * a]), gk_ref[...], p_ref[...])
    v_ref[...] = _dot(h, w_ref[:, o + 2 * a:o + 3 * a])
    gt_ref[...] = jax.nn.sigmoid(_dot(h, w_ref[:, o + 3 * a:]) + b_ref[...])


def _proj(x, g, w_in, b_gate, gq, gk, pmat):
    n, d = x.shape
    wtot = w_in.shape[1]
    tm = _token_tile(n)
    row = lambda w: pl.BlockSpec((tm, w), lambda i: (i, 0))
    return pl.pallas_call(
        _proj_kernel,
        grid=(n // tm,),
        in_specs=[
            row(d),
            _resident((1, d)),
            _resident((d, wtot)),
            _resident((1, 2 * d)),
            _resident((1, ATTN_WIDTH)),
            _resident((1, ATTN_WIDTH)),
            _resident(pmat.shape),
        ],
        out_specs=[row(POOL_WIDTH), row(ATTN_WIDTH), row(ATTN_WIDTH), row(ATTN_WIDTH), row(2 * d)],
        out_shape=[
            jax.ShapeDtypeStruct((n, POOL_WIDTH), _F32),
            jax.ShapeDtypeStruct((n, ATTN_WIDTH), _F32),
            jax.ShapeDtypeStruct((n, ATTN_WIDTH), _F32),
            jax.ShapeDtypeStruct((n, ATTN_WIDTH), _F32),
            jax.ShapeDtypeStruct((n, 2 * d), _F32),
        ],
        compiler_params=_params("parallel"),
        name="proj",
    )(x, g, w_in, b_gate, gq, gk, pmat)


def _moba_kernel(q_ref, k_ref, v_ref, o_ref, kb, vb, km, qb, bias, m_s, l_s, acc_s, *, nb, n_sel):
    i = pl.program_id(2)
    blk = MOBA_BLOCK

    @pl.when(i == 0)
    def _():
        k = k_ref[0]
        kb[...] = k.astype(_MXU)
        vb[...] = v_ref[0].astype(_MXU)
        km[...] = jnp.mean(k.reshape(nb, blk, LANES), axis=1)

    q2 = q_ref[0]
    lane = lax.broadcasted_iota(jnp.int32, (1, LANES), 1)
    col = lax.broadcasted_iota(jnp.int32, (1, nb), 1)
    past = col < i
    km_hi, km_lo = _split(km[...])
    for h in range(2):
        in_head = (lane >= HEAD_DIM * h) & (lane < HEAD_DIM * (h + 1))
        qh = jnp.where(in_head, q2, 0.0)
        q_hi, q_lo = _split(qh)
        sb = _dot_nt(q_hi, km_hi) + (_dot_nt(q_hi, km_lo) + _dot_nt(q_lo, km_hi))
        cur = jnp.where(past, sb, NEG)
        sel = jnp.zeros(cur.shape, jnp.bool_)
        for _ in range(n_sel):
            mx = jnp.max(cur, axis=1, keepdims=True)
            idx = jnp.min(jnp.where(cur == mx, col, nb), axis=1, keepdims=True)
            pick = col == idx
            sel = sel | pick
            cur = jnp.where(pick, -jnp.inf, cur)
        bias[h] = jnp.where(sel & past, 0.0, NEG)
        qb[h] = (qh * SCALE).astype(_MXU)

    row_i = lax.broadcasted_iota(jnp.int32, (blk, blk), 0)
    col_i = lax.broadcasted_iota(jnp.int32, (blk, blk), 1)
    causal = col_i <= row_i

    def process(j, diagonal):
        start = pl.multiple_of(j * blk, blk)
        kj = kb[pl.ds(start, blk), :]
        vj = vb[pl.ds(start, blk), :]
        for h in range(2):
            s = _dot_nt(qb[h], kj)
            if diagonal:
                s = jnp.where(causal, s, NEG)
                m_new = jnp.max(s, axis=1, keepdims=True)
                p = jnp.exp(s - m_new)
                l_s[h] = jnp.sum(p, axis=1, keepdims=True)
                acc_s[h] = _dot(p.astype(_MXU), vj)
            else:
                s = s + jnp.sum(jnp.where(col == j, bias[h], 0.0), axis=1, keepdims=True)
                m_old = m_s[h]
                m_new = jnp.maximum(m_old, jnp.max(s, axis=1, keepdims=True))
                alpha = jnp.exp(m_old - m_new)
                p = jnp.exp(s - m_new)
                l_s[h] = alpha * l_s[h] + jnp.sum(p, axis=1, keepdims=True)
                acc_s[h] = alpha * acc_s[h] + _dot(p.astype(_MXU), vj)
            m_s[h] = m_new

    process(i, True)

    def body(j, carry):
        process(j, False)
        return carry

    lax.fori_loop(0, i, body, 0)
    o_ref[0] = jnp.where(lane < HEAD_DIM, acc_s[0] / l_s[0], acc_s[1] / l_s[1])


def _moba_prompt(q, k, v):
    b, s, w = q.shape
    assert s % MOBA_BLOCK == 0 and w == ATTN_WIDTH
    nb = s // MOBA_BLOCK
    n_sel = min(MOBA_TOP_K, nb - 1)
    blk = MOBA_BLOCK
    qspec = pl.BlockSpec((1, blk, LANES), lambda bi, p, i: (bi, i, p))
    kvspec = pl.BlockSpec((1, s, LANES), lambda bi, p, i: (bi, 0, p))
    return pl.pallas_call(
        functools.partial(_moba_kernel, nb=nb, n_sel=n_sel),
        grid=(b, w // LANES, nb),
        in_specs=[qspec, kvspec, kvspec],
        out_specs=qspec,
        out_shape=jax.ShapeDtypeStruct((b, s, w), _F32),
        scratch_shapes=[
            pltpu.VMEM((s, LANES), _MXU),
            pltpu.VMEM((s, LANES), _MXU),
            pltpu.VMEM((nb, LANES), _F32),
            pltpu.VMEM((2, blk, LANES), _MXU),
            pltpu.VMEM((2, blk, nb), _F32),
            pltpu.VMEM((2, blk, 1), _F32),
            pltpu.VMEM((2, blk, 1), _F32),
            pltpu.VMEM((2, blk, LANES), _F32),
        ],
        compiler_params=_params("parallel", "parallel", "arbitrary"),
        name="moba_prompt",
    )(q, k, v)


DEC_CHUNK = 8
DEC_SLOTS = 4


def _column(ref, b):
    lane = lax.broadcasted_iota(jnp.int32, (1, ref.shape[1]), 1)
    return jnp.sum(jnp.where(lane == b, ref[...], 0.0), axis=1, keepdims=True)


def _dec_scores_kernel(pt_ref, qt_ref, k_hbm, w_ref, sel_ref, kbuf, qb, sem, *, n_pages, n_sel):
    b = pl.program_id(0)
    n_chunks = n_pages // DEC_CHUNK
    total = pl.num_programs(0) * n_chunks
    page_per_blk = MOBA_BLOCK // k_hbm.shape[3]
    nb = n_pages // page_per_blk

    def page_copy(g, pg):
        phys = pt_ref[g * DEC_CHUNK + pg]
        slot = g % DEC_SLOTS
        return pltpu.make_async_copy(k_hbm.at[phys], kbuf.at[slot, pg], sem.at[slot])

    def start_chunk(g):
        for pg in range(DEC_CHUNK):
            page_copy(g, pg).start()

    @pl.when(b == 0)
    def _():
        for g in range(DEC_SLOTS - 1):
            start_chunk(g)

    qb[...] = jnp.broadcast_to(_column(qt_ref, b), (ATTN_WIDTH, LANES)).reshape(N_HEADS, HEAD_DIM, LANES)

    def chunk_body(c, carry):
        g = b * n_chunks + c
        nxt = g + DEC_SLOTS - 1

        @pl.when(nxt < total)
        def _():
            start_chunk(nxt)

        for pg in range(DEC_CHUNK):
            page_copy(g, pg).wait()
        slot = g % DEC_SLOTS
        for pg in range(DEC_CHUNK):
            w = jnp.sum(kbuf[slot, pg] * qb[...], axis=1)
            w_ref[0, pl.ds(c * DEC_CHUNK + pg, 1)] = w[None]
        return carry

    lax.fori_loop(0, n_chunks, chunk_body, 0)

    wb = jnp.sum(w_ref[0].reshape(nb, page_per_blk, N_HEADS, LANES), axis=1)
    cur = jnp.sum(wb, axis=2, keepdims=True) * (1.0 / MOBA_BLOCK)
    blk_id = lax.broadcasted_iota(jnp.int32, cur.shape, 0)
    for r in range(n_sel):
        mx = jnp.max(cur, axis=0, keepdims=True)
        idx = jnp.min(jnp.where(cur == mx, blk_id, nb), axis=0, keepdims=True)
        sel_ref[0, r] = jnp.broadcast_to(idx[0], (N_HEADS, LANES))
        cur = jnp.where(blk_id == idx, -jnp.inf, cur)


def _dec_attend_kernel(pt_ref, sel_ref, w_ref, qt_ref, kt_ref, vt_ref, v_hbm, o_ref, vbuf, sem,
                       *, n_pages, n_sel):
    b = pl.program_id(0)
    page_per_blk = MOBA_BLOCK // v_hbm.shape[3]
    n_gather = n_sel * page_per_blk

    def copies(bb, slot):
        out = []
        for h in range(N_HEADS):
            for r in range(n_sel):
                blk = sel_ref[(bb * n_sel + r) * N_HEADS + h]
                for e in range(page_per_blk):
                    phys = pt_ref[bb * n_pages + blk * page_per_blk + e]
                    out.append(pltpu.make_async_copy(
                        v_hbm.at[phys, h], vbuf.at[slot, h, r * page_per_blk + e], sem.at[slot]))
        return out

    @pl.when(b == 0)
    def _():
        o_ref[...] = jnp.zeros(o_ref.shape, _F32)
        for c in copies(0, 0):
            c.start()

    @pl.when(b + 1 < pl.num_programs(0))
    def _():
        for c in copies(b + 1, (b + 1) % 2):
            c.start()

    slot = b % 2
    for c in copies(b, slot):
        c.wait()

    lane = lax.broadcasted_iota(jnp.int32, (1, o_ref.shape[1]), 1)
    qc = _column(qt_ref, b)
    kc = _column(kt_ref, b)
    vc = _column(vt_ref, b)
    s_own = jnp.sum((qc * kc).reshape(N_HEADS, HEAD_DIM, 1), axis=1) * SCALE
    for h in range(N_HEADS):
        rows = []
        for r in range(n_sel):
            blk = sel_ref[(b * n_sel + r) * N_HEADS + h]
            for e in range(page_per_blk):
                wp = w_ref[0, pl.ds(blk * page_per_blk + e, 1)][0]
                rows.append(wp[h:h + 1, :] * SCALE)
        so = s_own[h:h + 1, :]
        m = so
        for rw in rows:
            m = jnp.maximum(m, jnp.max(rw, axis=1, keepdims=True))
        p_own = jnp.exp(so - m)
        l = p_own
        acc = jnp.zeros((HEAD_DIM, rows[0].shape[1]), _F32)
        for g in range(n_gather):
            p = jnp.exp(rows[g] - m)
            l = l + jnp.sum(p, axis=1, keepdims=True)
            acc = acc + p * vbuf[slot, h, g]
        hs = slice(h * HEAD_DIM, (h + 1) * HEAD_DIM)
        o = (jnp.sum(acc, axis=1, keepdims=True) + p_own * vc[hs]) / l
        o_ref[hs, :] = jnp.where(lane == b, o, o_ref[hs, :])


def _moba_decode(q, k_new, v_new, cache_kt, cache_vt, page_table):
    db, n_pages = page_table.shape
    page = cache_kt.shape[3]
    assert MOBA_BLOCK % page == 0 and (n_pages * page) % MOBA_BLOCK == 0
    assert n_pages % DEC_CHUNK == 0 and db * (n_pages // DEC_CHUNK) >= DEC_SLOTS
    nb = n_pages * page // MOBA_BLOCK
    n_sel = min(MOBA_TOP_K, nb)
    assert n_sel > 0
    pt = page_table.reshape(-1)
    qt, kt, vt = q.T, k_new.T, v_new.T
    whole = lambda a: pl.BlockSpec(a.shape, lambda i, *_: (0,) * a.ndim)
    hbm = pl.BlockSpec(memory_space=pl.ANY)

    logits, sel = pl.pallas_call(
        functools.partial(_dec_scores_kernel, n_pages=n_pages, n_sel=n_sel),
        grid_spec=pltpu.PrefetchScalarGridSpec(
            num_scalar_prefetch=1,
            grid=(db,),
            in_specs=[whole(qt), hbm],
            out_specs=[
                pl.BlockSpec((1, n_pages, N_HEADS, page), lambda i, *_: (i, 0, 0, 0)),
                pl.BlockSpec((1, n_sel, N_HEADS, LANES), lambda i, *_: (i, 0, 0, 0)),
            ],
            scratch_shapes=[
                pltpu.VMEM((DEC_SLOTS, DEC_CHUNK, N_HEADS, HEAD_DIM, page), _F32),
                pltpu.VMEM((N_HEADS, HEAD_DIM, LANES), _F32),
                pltpu.SemaphoreType.DMA((DEC_SLOTS,)),
            ],
        ),
        out_shape=[
            jax.ShapeDtypeStruct((db, n_pages, N_HEADS, page), _F32),
            jax.ShapeDtypeStruct((db, n_sel, N_HEADS, LANES), jnp.int32),
        ],
        compiler_params=_params("arbitrary"),
        name="decode_scores",
    )(pt, qt, cache_kt)

    sel_flat = sel[:, :, :, 0].reshape(-1)
    n_gather = n_sel * (MOBA_BLOCK // page)
    out_t = pl.pallas_call(
        functools.partial(_dec_attend_kernel, n_pages=n_pages, n_sel=n_sel),
        grid_spec=pltpu.PrefetchScalarGridSpec(
            num_scalar_prefetch=2,
            grid=(db,),
            in_specs=[
                pl.BlockSpec((1, n_pages, N_HEADS, page), lambda i, *_: (i, 0, 0, 0)),
                whole(qt), whole(kt), whole(vt), hbm,
            ],
            out_specs=pl.BlockSpec((ATTN_WIDTH, db), lambda i, *_: (0, 0)),
            scratch_shapes=[
                pltpu.VMEM((2, N_HEADS, n_gather, HEAD_DIM, page), _F32),
                pltpu.SemaphoreType.DMA((2,)),
            ],
        ),
        out_shape=jax.ShapeDtypeStruct((ATTN_WIDTH, db), _F32),
        compiler_params=_params("arbitrary"),
        name="decode_attend",
    )(pt, sel_flat, logits, qt, kt, vt, cache_vt)
    return out_t.T


def _pool_branch(d_groups, wpg_ref, sp_ref):
    outs = []
    for g, d in enumerate(d_groups):
        cols = slice(g * POOL_GROUP_WIDTH, (g + 1) * POOL_GROUP_WIDTH)
        outs.append(_dot(d.astype(_MXU), wpg_ref[g]) * sp_ref[:, cols])
    return jnp.concatenate(outs, axis=1)


def _mix_tail(x, pool, attn, gt_ref, wbp_ref, wba_ref, wo_ref, o_ref):
    d = x.shape[1]
    m = (gt_ref[:, :d] * _dot(pool.astype(_MXU), wbp_ref[...])
         + gt_ref[:, d:] * _dot(attn.astype(_MXU), wba_ref[...]))
    o_ref[...] = x + _dot(m.astype(_MXU), wo_ref[...])


def _mix_prompt_kernel(x_ref, at_ref, u_ref, hist_ref, gt_ref, wpg_ref, sp_ref, wbp_ref, wba_ref,
                       wo_ref, o_ref, ext, *, seq):
    tm = u_ref.shape[0]
    pos0 = (pl.program_id(0) * tm) % seq
    ext[0:HIST_ROWS, :] = jnp.where(pos0 > 0, hist_ref[...], 0.0)
    ext[HIST_ROWS:, :] = u_ref[...]
    pos = pos0 + lax.broadcasted_iota(jnp.int32, (tm, 1), 0)
    d_groups = []
    for g, w in enumerate(POOL_WINDOWS):
        cols = slice(g * POOL_GROUP_WIDTH, (g + 1) * POOL_GROUP_WIDTH)
        cur = ext[HIST_ROWS:, cols]
        total = cur
        for j in range(1, w):
            total = total + ext[pl.ds(HIST_ROWS - j, tm), cols]
        cnt = jnp.minimum(w, pos + 1).astype(_F32)
        d_groups.append(total / cnt - cur)
    pool = _pool_branch(d_groups, wpg_ref, sp_ref)
    _mix_tail(x_ref[...], pool, at_ref[...], gt_ref, wbp_ref, wba_ref, wo_ref, o_ref)


def _mix_decode_kernel(x_ref, at_ref, u_ref, hist_ref, gt_ref, wpg_ref, sp_ref, wbp_ref, wba_ref,
                       wo_ref, o_ref, *, past):
    d_groups = []
    for g, w in enumerate(POOL_WINDOWS):
        cols = slice(g * POOL_GROUP_WIDTH, (g + 1) * POOL_GROUP_WIDTH)
        cur = u_ref[:, cols]
        total = cur
        for j in range(1, w):
            total = total + hist_ref[POOL_HIST - j, :, cols]
        d_groups.append(total / float(min(w, past + 1)) - cur)
    pool = _pool_branch(d_groups, wpg_ref, sp_ref)
    _mix_tail(x_ref[...], pool, at_ref[...], gt_ref, wbp_ref, wba_ref, wo_ref, o_ref)


def _mix(x, attn, u, hist, gates, wpg, sp, wbp, wba, wo, *, seq=None, past=None):
    n, d = x.shape
    tm = _token_tile(n)
    row = lambda w: pl.BlockSpec((tm, w), lambda i: (i, 0))
    if seq is not None:
        assert seq % tm == 0 and tm % HIST_ROWS == 0
        body = functools.partial(_mix_prompt_kernel, seq=seq)
        hist_spec = pl.BlockSpec(
            (HIST_ROWS, POOL_WIDTH), lambda i: (jnp.maximum(i * (tm // HIST_ROWS) - 1, 0), 0))
        scratch = [pltpu.VMEM((tm + HIST_ROWS, POOL_WIDTH), _F32)]
    else:
        body = functools.partial(_mix_decode_kernel, past=past)
        hist_spec = pl.BlockSpec((POOL_HIST, tm, POOL_WIDTH), lambda i: (0, i, 0))
        scratch = []
    return pl.pallas_call(
        body,
        grid=(n // tm,),
        in_specs=[
            row(d), row(ATTN_WIDTH), row(POOL_WIDTH), hist_spec, row(2 * d),
            _resident(wpg.shape), _resident(sp.shape), _resident(wbp.shape),
            _resident(wba.shape), _resident(wo.shape),
        ],
        out_specs=row(d),
        out_shape=jax.ShapeDtypeStruct((n, d), _F32),
        scratch_shapes=scratch,
        compiler_params=_params("parallel"),
        name="mix",
    )(x, attn, u, hist, gates, wpg, sp, wbp, wba, wo)


def kernel(x_prompt, x_sample, cache_k, cache_v, state_pool, page_table, g_ffn1, w1_ffn1, w3_ffn1,
           w2_ffn1, g_mix, w_in, b_gate, g_q, g_k, w_pool_grp, s_pool, w_branch_pool,
           w_branch_attn, w_out, g_ffn2, w1_ffn2, w3_ffn2, w2_ffn2):
    assert w_in.shape[0] == 1 and x_sample.shape[1] == 1
    b, s, d = x_prompt.shape
    db = x_sample.shape[0]
    page = cache_k.shape[2]
    past = page_table.shape[1] * page
    mx = lambda w: w[0].astype(_MXU)

    ffn1 = (g_ffn1, mx(w1_ffn1), mx(w3_ffn1), mx(w2_ffn1))
    ffn2 = (g_ffn2, mx(w1_ffn2), mx(w3_ffn2), mx(w2_ffn2))
    head = jnp.arange(2 * LANES) // HEAD_DIM
    pmat = ((head[:, None] == head[None, :]).astype(_F32) / HEAD_DIM).astype(_MXU)
    proj = (g_mix, mx(w_in), b_gate, jnp.tile(g_q, (1, N_HEADS)), jnp.tile(g_k, (1, N_HEADS)), pmat)
    post = (mx(w_pool_grp), s_pool, mx(w_branch_pool), mx(w_branch_attn), mx(w_out))

    xp = _ffn(x_prompt.reshape(b * s, d), *ffn1)
    u, q, k, v, gates = _proj(xp, *proj)
    shape3 = (b, s, ATTN_WIDTH)
    attn = _moba_prompt(q.reshape(shape3), k.reshape(shape3), v.reshape(shape3))
    xp = _mix(xp, attn.reshape(b * s, ATTN_WIDTH), u, u, gates, *post, seq=s)
    yp = _ffn(xp, *ffn2).reshape(b, s, d)

    xs = _ffn(x_sample.reshape(db, d), *ffn1)
    us, qs, ks, vs, gates_s = _proj(xs, *proj)
    cache_kt = jnp.transpose(cache_k[0], (0, 2, 3, 1))
    cache_vt = jnp.transpose(cache_v[0], (0, 2, 3, 1))
    attn_s = _moba_decode(qs, ks, vs, cache_kt, cache_vt, page_table)
    hist = jnp.transpose(state_pool[0], (1, 0, 2))
    xs = _mix(xs, attn_s, us, hist, gates_s, *post, past=past)
    ys = _ffn(xs, *ffn2).reshape(db, 1, d)

    heads = (N_HEADS, HEAD_DIM)
    pool_p = u.reshape(b, s, POOL_WIDTH)[:, s - POOL_HIST:]
    pool_s = jnp.concatenate([state_pool[0, :, 1:], us[:, None]], axis=1)
    return (yp, ys,
            k.reshape(1, b, s, *heads), v.reshape(1, b, s, *heads), pool_p[None],
            ks.reshape(1, db, 1, *heads), vs.reshape(1, db, 1, *heads), pool_s[None])
```

```python
import functools

import jax
import jax.numpy as jnp
from jax import lax
from jax.experimental import pallas as pl
from jax.experimental.pallas import tpu as pltpu

N_HEADS = 8
HEAD_DIM = 64
ATTN_WIDTH = N_HEADS * HEAD_DIM
MOBA_BLOCK = 256
MOBA_TOP_K = 3
POOL_WINDOWS = (2, 4, 8, 16)
POOL_GROUP_WIDTH = 128
POOL_WIDTH = len(POOL_WINDOWS) * POOL_GROUP_WIDTH
POOL_HIST = max(POOL_WINDOWS) - 1
HIST_ROWS = 16
EPS = 1e-6
NEG = -1e30
SCALE = HEAD_DIM ** -0.5

LANES = 128
TOKEN_TILE = 512
VMEM_LIMIT = 56 * 1024 * 1024

_MXU = jnp.bfloat16
_F32 = jnp.float32


def _dot(a, b):
    return jnp.dot(a, b, preferred_element_type=_F32)


def _dot_nt(a, b):
    return lax.dot_general(a, b, (((1,), (1,)), ((), ())), preferred_element_type=_F32)


def _split(a):
    hi = a.astype(_MXU)
    lo = (a - hi.astype(_F32)).astype(_MXU)
    return hi, lo


def _rms(x, g):
    ms = jnp.mean(x * x, axis=-1, keepdims=True)
    return x * lax.rsqrt(ms + EPS) * g


def _params(*sem):
    return pltpu.CompilerParams(dimension_semantics=sem, vmem_limit_bytes=VMEM_LIMIT)


def _resident(shape):
    return pl.BlockSpec(shape, lambda *_: (0,) * len(shape), pipeline_mode=pl.Buffered(1))


def _token_tile(n):
    return TOKEN_TILE if n % TOKEN_TILE == 0 else n


def _ffn_kernel(x_ref, g_ref, w1_ref, w3_ref, w2_ref, o_ref):
    x = x_ref[...]
    h = _rms(x, g_ref[...]).astype(_MXU)
    a = _dot(h, w1_ref[...])
    b = _dot(h, w3_ref[...])
    act = (a * jax.nn.sigmoid(a) * b).astype(_MXU)
    o_ref[...] = x + 0.5 * _dot(act, w2_ref[...])


def _ffn(x, g, w1, w3, w2):
    n, d = x.shape
    f = w1.shape[1]
    tm = _token_tile(n)
    return pl.pallas_call(
        _ffn_kernel,
        grid=(n // tm,),
        in_specs=[
            pl.BlockSpec((tm, d), lambda i: (i, 0)),
            _resident((1, d)),
            _resident((d, f)),
            _resident((d, f)),
            _resident((f, d)),
        ],
        out_specs=pl.BlockSpec((tm, d), lambda i: (i, 0)),
        out_shape=jax.ShapeDtypeStruct((n, d), _F32),
        compiler_params=_params("parallel"),
        name="ffn",
    )(x, g, w1, w3, w2)


def _head_rms(z, g, pmat):
    z2 = z * z
    half = pmat.shape[0]
    ms = []
    for c in range(z.shape[1] // half):
        hi, lo = _split(z2[:, c * half:(c + 1) * half])
        ms.append(_dot(hi, pmat) + _dot(lo, pmat))
    ms = jnp.concatenate(ms, axis=1)
    return z * lax.rsqrt(ms + EPS) * g


def _proj_kernel(x_ref, g_ref, w_ref, b_ref, gq_ref, gk_ref, p_ref,
                 u_ref, q_ref, k_ref, v_ref, gt_ref, kt_ref, vt_ref):
    h = _rms(x_ref[...], g_ref[...]).astype(_MXU)
    a = ATTN_WIDTH
    o = POOL_WIDTH
    u_ref[...] = _dot(h, w_ref[:, 0:o])
    q_ref[...] = _head_rms(_dot(h, w_ref[:, o:o + a]), gq_ref[...], p_ref[...])
    k = _head_rms(_dot(h, w_ref[:, o + a:o + 2 * a]), gk_ref[...], p_ref[...])
    v = _dot(h, w_ref[:, o + 2 * a:o + 3 * a])
    k_ref[...] = k
    v_ref[...] = v
    kt_ref[0] = k.T
    vt_ref[0] = v.T
    gt_ref[...] = jax.nn.sigmoid(_dot(h, w_ref[:, o + 3 * a:]) + b_ref[...])


def _proj(x, g, w_in, b_gate, gq, gk, pmat, *, seq):
    n, d = x.shape
    wtot = w_in.shape[1]
    tm = _token_tile(n)
    assert seq % tm == 0 and n % seq == 0
    tiles_per_seq = seq // tm
    row = lambda w: pl.BlockSpec((tm, w), lambda i: (i, 0))
    tok_minor = pl.BlockSpec((1, ATTN_WIDTH, tm), lambda i: (i // tiles_per_seq, 0, i % tiles_per_seq))
    return pl.pallas_call(
        _proj_kernel,
        grid=(n // tm,),
        in_specs=[
            row(d),
            _resident((1, d)),
            _resident((d, wtot)),
            _resident((1, 2 * d)),
            _resident((1, ATTN_WIDTH)),
            _resident((1, ATTN_WIDTH)),
            _resident(pmat.shape),
        ],
        out_specs=[row(POOL_WIDTH), row(ATTN_WIDTH), row(ATTN_WIDTH), row(ATTN_WIDTH), row(2 * d),
                   tok_minor, tok_minor],
        out_shape=[
            jax.ShapeDtypeStruct((n, POOL_WIDTH), _F32),
            jax.ShapeDtypeStruct((n, ATTN_WIDTH), _F32),
            jax.ShapeDtypeStruct((n, ATTN_WIDTH), _F32),
            jax.ShapeDtypeStruct((n, ATTN_WIDTH), _F32),
            jax.ShapeDtypeStruct((n, 2 * d), _F32),
            jax.ShapeDtypeStruct((n // seq, ATTN_WIDTH, seq), _F32),
            jax.ShapeDtypeStruct((n // seq, ATTN_WIDTH, seq), _F32),
        ],
        compiler_params=_params("parallel"),
        name="proj",
    )(x, g, w_in, b_gate, gq, gk, pmat)


Q_ROWS = 2 * MOBA_BLOCK
PREP_ROWS = 4 * MOBA_BLOCK
LOG2E = 1.4426950408889634


def _moba_kernel(q_ref, k_ref, v_ref, o_ref, kaug, vaug, qaug, km, s_buf, mrun, mb, acc_s, *, nb, n_sel):
    step = pl.program_id(2)
    blk = MOBA_BLOCK
    blk_shift = blk.bit_length() - 1
    half = HEAD_DIM
    lane = lax.broadcasted_iota(jnp.int32, (1, LANES), 1)
    in_head = [(lane >= half * h) & (lane < half * (h + 1)) for h in range(2)]

    @pl.when(step == 0)
    def _():
        km[...] = jnp.mean(k_ref[0].reshape(nb, blk, LANES), axis=1)
        km_hi, km_lo = _split(km[...])

        def prep(c, carry):
            r0 = pl.multiple_of(c * PREP_ROWS, PREP_ROWS)
            rows = pl.ds(r0, PREP_ROWS)
            k = k_ref[0, rows, :]
            v = v_ref[0, rows, :]
            q2 = q_ref[0, rows, :]
            key_blk = lax.shift_right_logical(
                r0 + lax.broadcasted_iota(jnp.int32, (PREP_ROWS, LANES), 0), blk_shift)
            lane_s = lax.broadcasted_iota(jnp.int32, (PREP_ROWS, LANES), 1)
            blk_id = lax.broadcasted_iota(jnp.int32, (nb, PREP_ROWS), 0)
            own = lax.shift_right_logical(
                r0 + lax.broadcasted_iota(jnp.int32, (nb, PREP_ROWS), 1), blk_shift)
            past = blk_id < own
            for h in range(2):
                onehot = (lane_s - half * (1 - h) == key_blk).astype(_F32)
                kaug[h, rows, :] = jnp.where(in_head[h], k, onehot).astype(_MXU)
                vaug[h, rows, :] = jnp.where(in_head[h], v, 1.0).astype(_MXU)
                qh = jnp.where(in_head[h], q2, 0.0)
                q_hi, q_lo = _split(qh)
                sb = _dot_nt(km_hi, q_hi) + (_dot_nt(km_lo, q_hi) + _dot_nt(km_hi, q_lo))
                cur = jnp.where(past, sb, NEG)
                sel = jnp.zeros(cur.shape, jnp.bool_)
                for _ in range(n_sel):
                    mx = jnp.max(cur, axis=0, keepdims=True)
                    idx = jnp.min(jnp.where(cur == mx, blk_id, nb), axis=0, keepdims=True)
                    pick = blk_id == idx
                    sel = sel | pick
                    cur = jnp.where(pick, -jnp.inf, cur)
                bias_t = jnp.where((sel & past) | (blk_id == own), 0.0, NEG)
                before = half * (1 - h)
                parts = [bias_t, jnp.zeros((LANES - before - nb, PREP_ROWS), _F32)]
                if before:
                    parts.insert(0, jnp.zeros((before, PREP_ROWS), _F32))
                bias_lanes = jnp.concatenate(parts, axis=0).T
                qaug[h, rows, :] = jnp.where(in_head[h], qh * (SCALE * LOG2E), bias_lanes).astype(_MXU)
            return carry

        lax.fori_loop(0, k_ref.shape[1] // PREP_ROWS, prep, 0)

    def rows_of(ref, h, t):
        return ref[h, pl.ds(pl.multiple_of(t * Q_ROWS, Q_ROWS), Q_ROWS), :]

    def lane_max(s):
        m = s[:, :LANES]
        for c in range(1, s.shape[1] // LANES):
            m = jnp.maximum(m, s[:, c * LANES:(c + 1) * LANES])
        return m

    row_i = lax.broadcasted_iota(jnp.int32, (Q_ROWS, Q_ROWS), 0)
    col_i = lax.broadcasted_iota(jnp.int32, (Q_ROWS, Q_ROWS), 1)
    for h in range(2):
        s = jnp.where(col_i <= row_i, _dot_nt(rows_of(qaug, h, step), rows_of(kaug, h, step)), NEG)
        s_buf[h, step] = s
        mrun[h] = lane_max(s)

    def scores(t, carry):
        for h in range(2):
            s = _dot_nt(rows_of(qaug, h, step), rows_of(kaug, h, t))
            s_buf[h, t] = s
            mrun[h] = jnp.maximum(mrun[h], lane_max(s))
        return carry

    lax.fori_loop(0, step, scores, 0)

    for h in range(2):
        mb[h] = jnp.broadcast_to(jnp.max(mrun[h], axis=1, keepdims=True), (Q_ROWS, LANES))
        acc_s[h] = jnp.zeros((Q_ROWS, LANES), _F32)

    def attend(t, carry):
        for h in range(2):
            m = mb[h]
            p = jnp.exp2(s_buf[h, t] - jnp.concatenate([m] * (Q_ROWS // LANES), axis=1)).astype(_MXU)
            acc_s[h] = acc_s[h] + _dot(p, rows_of(vaug, h, t))
        return carry

    lax.fori_loop(0, step + 1, attend, 0)

    num = jnp.where(in_head[0], acc_s[0], acc_s[1])
    den = pltpu.roll(jnp.where(in_head[0], acc_s[1], acc_s[0]), half, axis=1)
    o_ref[0] = num / den


def _moba_prompt(q, k, v):
    b, s, w = q.shape
    assert w == ATTN_WIDTH and s % PREP_ROWS == 0 and MOBA_BLOCK & (MOBA_BLOCK - 1) == 0
    nb = s // MOBA_BLOCK
    assert nb <= HEAD_DIM and nb % 8 == 0
    n_sel = min(MOBA_TOP_K, nb - 1)
    whole = pl.BlockSpec((1, s, LANES), lambda bi, p, i: (bi, 0, p))
    return pl.pallas_call(
        functools.partial(_moba_kernel, nb=nb, n_sel=n_sel),
        grid=(b, w // LANES, s // Q_ROWS),
        in_specs=[whole, whole, whole],
        out_specs=pl.BlockSpec((1, Q_ROWS, LANES), lambda bi, p, i: (bi, i, p)),
        out_shape=jax.ShapeDtypeStruct((b, s, w), _F32),
        scratch_shapes=[
            pltpu.VMEM((2, s, LANES), _MXU),
            pltpu.VMEM((2, s, LANES), _MXU),
            pltpu.VMEM((2, s, LANES), _MXU),
            pltpu.VMEM((nb, LANES), _F32),
            pltpu.VMEM((2, s // Q_ROWS, Q_ROWS, Q_ROWS), _F32),
            pltpu.VMEM((2, Q_ROWS, LANES), _F32),
            pltpu.VMEM((2, Q_ROWS, LANES), _F32),
            pltpu.VMEM((2, Q_ROWS, LANES), _F32),
        ],
        compiler_params=_params("parallel", "parallel", "arbitrary"),
        name="moba_prompt",
    )(q, k, v)


DEC_CHUNK = 8
DEC_SLOTS = 4


def _column(ref, b):
    lane = lax.broadcasted_iota(jnp.int32, (1, ref.shape[1]), 1)
    return jnp.sum(jnp.where(lane == b, ref[...], 0.0), axis=1, keepdims=True)


def _dec_scores_kernel(pt_ref, qt_ref, k_hbm, w_ref, sel_ref, kbuf, qb, sem, *, n_pages, n_sel):
    b = pl.program_id(0)
    n_chunks = n_pages // DEC_CHUNK
    total = pl.num_programs(0) * n_chunks
    page_per_blk = MOBA_BLOCK // k_hbm.shape[3]
    nb = n_pages // page_per_blk

    def page_copy(g, pg):
        phys = pt_ref[g * DEC_CHUNK + pg]
        slot = g % DEC_SLOTS
        return pltpu.make_async_copy(k_hbm.at[phys], kbuf.at[slot, pg], sem.at[slot])

    def start_chunk(g):
        for pg in range(DEC_CHUNK):
            page_copy(g, pg).start()

    @pl.when(b == 0)
    def _():
        for g in range(DEC_SLOTS - 1):
            start_chunk(g)

    qb[...] = jnp.broadcast_to(_column(qt_ref, b), (ATTN_WIDTH, LANES)).reshape(N_HEADS, HEAD_DIM, LANES)

    def chunk_body(c, carry):
        g = b * n_chunks + c
        nxt = g + DEC_SLOTS - 1

        @pl.when(nxt < total)
        def _():
            start_chunk(nxt)

        for pg in range(DEC_CHUNK):
            page_copy(g, pg).wait()
        slot = g % DEC_SLOTS
        for pg in range(DEC_CHUNK):
            w = jnp.sum(kbuf[slot, pg] * qb[...], axis=1)
            w_ref[0, pl.ds(c * DEC_CHUNK + pg, 1)] = w[None]
        return carry

    lax.fori_loop(0, n_chunks, chunk_body, 0)

    wb = jnp.sum(w_ref[0].reshape(nb, page_per_blk, N_HEADS, LANES), axis=1)
    cur = jnp.sum(wb, axis=2, keepdims=True) * (1.0 / MOBA_BLOCK)
    blk_id = lax.broadcasted_iota(jnp.int32, cur.shape, 0)
    for r in range(n_sel):
        mx = jnp.max(cur, axis=0, keepdims=True)
        idx = jnp.min(jnp.where(cur == mx, blk_id, nb), axis=0, keepdims=True)
        sel_ref[0, r] = jnp.broadcast_to(idx[0], (N_HEADS, LANES))
        cur = jnp.where(blk_id == idx, -jnp.inf, cur)


def _dec_attend_kernel(pt_ref, sel_ref, w_ref, qt_ref, kt_ref, vt_ref, v_hbm, o_ref, vbuf, sem,
                       *, n_pages, n_sel):
    b = pl.program_id(0)
    page_per_blk = MOBA_BLOCK // v_hbm.shape[3]
    n_gather = n_sel * page_per_blk

    def copies(bb, slot):
        out = []
        for h in range(N_HEADS):
            for r in range(n_sel):
                blk = sel_ref[(bb * n_sel + r) * N_HEADS + h]
                for e in range(page_per_blk):
                    phys = pt_ref[bb * n_pages + blk * page_per_blk + e]
                    out.append(pltpu.make_async_copy(
                        v_hbm.at[phys, h], vbuf.at[slot, h, r * page_per_blk + e], sem.at[slot]))
        return out

    @pl.when(b == 0)
    def _():
        o_ref[...] = jnp.zeros(o_ref.shape, _F32)
        for c in copies(0, 0):
            c.start()

    @pl.when(b + 1 < pl.num_programs(0))
    def _():
        for c in copies(b + 1, (b + 1) % 2):
            c.start()

    slot = b % 2
    for c in copies(b, slot):
        c.wait()

    lane = lax.broadcasted_iota(jnp.int32, (1, o_ref.shape[1]), 1)
    qc = _column(qt_ref, b)
    kc = _column(kt_ref, b)
    vc = _column(vt_ref, b)
    s_own = jnp.sum((qc * kc).reshape(N_HEADS, HEAD_DIM, 1), axis=1) * SCALE
    for h in range(N_HEADS):
        rows = []
        for r in range(n_sel):
            blk = sel_ref[(b * n_sel + r) * N_HEADS + h]
            for e in range(page_per_blk):
                wp = w_ref[0, pl.ds(blk * page_per_blk + e, 1)][0]
                rows.append(wp[h:h + 1, :] * SCALE)
        so = s_own[h:h + 1, :]
        m = so
        for rw in rows:
            m = jnp.maximum(m, jnp.max(rw, axis=1, keepdims=True))
        p_own = jnp.exp(so - m)
        l = p_own
        acc = jnp.zeros((HEAD_DIM, rows[0].shape[1]), _F32)
        for g in range(n_gather):
            p = jnp.exp(rows[g] - m)
            l = l + jnp.sum(p, axis=1, keepdims=True)
            acc = acc + p * vbuf[slot, h, g]
        hs = slice(h * HEAD_DIM, (h + 1) * HEAD_DIM)
        o = (jnp.sum(acc, axis=1, keepdims=True) + p_own * vc[hs]) / l
        o_ref[hs, :] = jnp.where(lane == b, o, o_ref[hs, :])


def _moba_decode(qt, kt, vt, cache_kt, cache_vt, page_table):
    db, n_pages = page_table.shape
    page = cache_kt.shape[3]
    assert MOBA_BLOCK % page == 0 and (n_pages * page) % MOBA_BLOCK == 0
    assert n_pages % DEC_CHUNK == 0 and db * (n_pages // DEC_CHUNK) >= DEC_SLOTS
    nb = n_pages * page // MOBA_BLOCK
    n_sel = min(MOBA_TOP_K, nb)
    assert n_sel > 0
    pt = page_table.reshape(-1)
    whole = lambda a: pl.BlockSpec(a.shape, lambda i, *_: (0,) * a.ndim)
    hbm = pl.BlockSpec(memory_space=pl.ANY)

    logits, sel = pl.pallas_call(
        functools.partial(_dec_scores_kernel, n_pages=n_pages, n_sel=n_sel),
        grid_spec=pltpu.PrefetchScalarGridSpec(
            num_scalar_prefetch=1,
            grid=(db,),
            in_specs=[whole(qt), hbm],
            out_specs=[
                pl.BlockSpec((1, n_pages, N_HEADS, page), lambda i, *_: (i, 0, 0, 0)),
                pl.BlockSpec((1, n_sel, N_HEADS, LANES), lambda i, *_: (i, 0, 0, 0)),
            ],
            scratch_shapes=[
                pltpu.VMEM((DEC_SLOTS, DEC_CHUNK, N_HEADS, HEAD_DIM, page), _F32),
                pltpu.VMEM((N_HEADS, HEAD_DIM, LANES), _F32),
                pltpu.SemaphoreType.DMA((DEC_SLOTS,)),
            ],
        ),
        out_shape=[
            jax.ShapeDtypeStruct((db, n_pages, N_HEADS, page), _F32),
            jax.ShapeDtypeStruct((db, n_sel, N_HEADS, LANES), jnp.int32),
        ],
        compiler_params=_params("arbitrary"),
        name="decode_scores",
    )(pt, qt, cache_kt)

    sel_flat = sel[:, :, :, 0].reshape(-1)
    n_gather = n_sel * (MOBA_BLOCK // page)
    out_t = pl.pallas_call(
        functools.partial(_dec_attend_kernel, n_pages=n_pages, n_sel=n_sel),
        grid_spec=pltpu.PrefetchScalarGridSpec(
            num_scalar_prefetch=2,
            grid=(db,),
            in_specs=[
                pl.BlockSpec((1, n_pages, N_HEADS, page), lambda i, *_: (i, 0, 0, 0)),
                whole(qt), whole(kt), whole(vt), hbm,
            ],
            out_specs=pl.BlockSpec((ATTN_WIDTH, db), lambda i, *_: (0, 0)),
            scratch_shapes=[
                pltpu.VMEM((2, N_HEADS, n_gather, HEAD_DIM, page), _F32),
                pltpu.SemaphoreType.DMA((2,)),
            ],
        ),
        out_shape=jax.ShapeDtypeStruct((ATTN_WIDTH, db), _F32),
        compiler_params=_params("arbitrary"),
        name="decode_attend",
    )(pt, sel_flat, logits, qt, kt, vt, cache_vt)
    return out_t.T


def _pool_branch(d_groups, wpg_ref, sp_ref):
    outs = []
    for g, d in enumerate(d_groups):
        cols = slice(g * POOL_GROUP_WIDTH, (g + 1) * POOL_GROUP_WIDTH)
        outs.append(_dot(d.astype(_MXU), wpg_ref[g]) * sp_ref[:, cols])
    return jnp.concatenate(outs, axis=1)


def _mix_tail(x, pool, attn, gt_ref, wbp_ref, wba_ref, wo_ref, o_ref):
    d = x.shape[1]
    m = (gt_ref[:, :d] * _dot(pool.astype(_MXU), wbp_ref[...])
         + gt_ref[:, d:] * _dot(attn.astype(_MXU), wba_ref[...]))
    o_ref[...] = x + _dot(m.astype(_MXU), wo_ref[...])


def _mix_prompt_kernel(x_ref, at_ref, u_ref, hist_ref, gt_ref, wpg_ref, sp_ref, wbp_ref, wba_ref,
                       wo_ref, o_ref, ext, *, seq):
    tm = u_ref.shape[0]
    pos0 = (pl.program_id(0) * tm) % seq
    ext[0:HIST_ROWS, :] = jnp.where(pos0 > 0, hist_ref[...], 0.0)
    ext[HIST_ROWS:, :] = u_ref[...]
    pos = pos0 + lax.broadcasted_iota(jnp.int32, (tm, 1), 0)
    d_groups = []
    for g, w in enumerate(POOL_WINDOWS):
        cols = slice(g * POOL_GROUP_WIDTH, (g + 1) * POOL_GROUP_WIDTH)
        cur = ext[HIST_ROWS:, cols]
        total = cur
        for j in range(1, w):
            total = total + ext[pl.ds(HIST_ROWS - j, tm), cols]
        cnt = jnp.minimum(w, pos + 1).astype(_F32)
        d_groups.append(total / cnt - cur)
    pool = _pool_branch(d_groups, wpg_ref, sp_ref)
    _mix_tail(x_ref[...], pool, at_ref[...], gt_ref, wbp_ref, wba_ref, wo_ref, o_ref)


def _mix_decode_kernel(x_ref, at_ref, u_ref, hist_ref, gt_ref, wpg_ref, sp_ref, wbp_ref, wba_ref,
                       wo_ref, o_ref, *, past):
    d_groups = []
    for g, w in enumerate(POOL_WINDOWS):
        cols = slice(g * POOL_GROUP_WIDTH, (g + 1) * POOL_GROUP_WIDTH)
        cur = u_ref[:, cols]
        total = cur
        for j in range(1, w):
            total = total + hist_ref[POOL_HIST - j, :, cols]
        d_groups.append(total / float(min(w, past + 1)) - cur)
    pool = _pool_branch(d_groups, wpg_ref, sp_ref)
    _mix_tail(x_ref[...], pool, at_ref[...], gt_ref, wbp_ref, wba_ref, wo_ref, o_ref)


def _mix(x, attn, u, hist, gates, wpg, sp, wbp, wba, wo, *, seq=None, past=None):
    n, d = x.shape
    tm = _token_tile(n)
    row = lambda w: pl.BlockSpec((tm, w), lambda i: (i, 0))
    if seq is not None:
        assert seq % tm == 0 and tm % HIST_ROWS == 0
        body = functools.partial(_mix_prompt_kernel, seq=seq)
        hist_spec = pl.BlockSpec(
            (HIST_ROWS, POOL_WIDTH), lambda i: (jnp.maximum(i * (tm // HIST_ROWS) - 1, 0), 0))
        scratch = [pltpu.VMEM((tm + HIST_ROWS, POOL_WIDTH), _F32)]
    else:
        body = functools.partial(_mix_decode_kernel, past=past)
        hist_spec = pl.BlockSpec((POOL_HIST, tm, POOL_WIDTH), lambda i: (0, i, 0))
        scratch = []
    return pl.pallas_call(
        body,
        grid=(n // tm,),
        in_specs=[
            row(d), row(ATTN_WIDTH), row(POOL_WIDTH), hist_spec, row(2 * d),
            _resident(wpg.shape), _resident(sp.shape), _resident(wbp.shape),
            _resident(wba.shape), _resident(wo.shape),
        ],
        out_specs=row(d),
        out_shape=jax.ShapeDtypeStruct((n, d), _F32),
        scratch_shapes=scratch,
        compiler_params=_params("parallel"),
        name="mix",
    )(x, attn, u, hist, gates, wpg, sp, wbp, wba, wo)


def kernel(x_prompt, x_sample, cache_k, cache_v, state_pool, page_table, g_ffn1, w1_ffn1, w3_ffn1,
           w2_ffn1, g_mix, w_in, b_gate, g_q, g_k, w_pool_grp, s_pool, w_branch_pool,
           w_branch_attn, w_out, g_ffn2, w1_ffn2, w3_ffn2, w2_ffn2):
    assert w_in.shape[0] == 1 and x_sample.shape[1] == 1
    b, s, d = x_prompt.shape
    db = x_sample.shape[0]
    page = cache_k.shape[2]
    past = page_table.shape[1] * page
    mx = lambda w: w[0].astype(_MXU)

    ffn1 = (g_ffn1, mx(w1_ffn1), mx(w3_ffn1), mx(w2_ffn1))
    ffn2 = (g_ffn2, mx(w1_ffn2), mx(w3_ffn2), mx(w2_ffn2))
    head = jnp.arange(2 * LANES) // HEAD_DIM
    pmat = ((head[:, None] == head[None, :]).astype(_F32) / HEAD_DIM).astype(_MXU)
    proj = (g_mix, mx(w_in), b_gate, jnp.tile(g_q, (1, N_HEADS)), jnp.tile(g_k, (1, N_HEADS)), pmat)
    post = (mx(w_pool_grp), s_pool, mx(w_branch_pool), mx(w_branch_attn), mx(w_out))

    xp = _ffn(x_prompt.reshape(b * s, d), *ffn1)
    u, q, k, v, gates, kt, vt = _proj(xp, *proj, seq=s)
    shape3 = (b, s, ATTN_WIDTH)
    attn = _moba_prompt(q.reshape(shape3), k.reshape(shape3), v.reshape(shape3))
    xp = _mix(xp, attn.reshape(b * s, ATTN_WIDTH), u, u, gates, *post, seq=s)
    yp = _ffn(xp, *ffn2).reshape(b, s, d)

    xs = _ffn(x_sample.reshape(db, d), *ffn1)
    us, qs, _, _, gates_s, kst, vst = _proj(xs, *proj, seq=db)
    cache_kt = jnp.transpose(cache_k[0], (0, 2, 3, 1))
    cache_vt = jnp.transpose(cache_v[0], (0, 2, 3, 1))
    attn_s = _moba_decode(qs.T, kst[0], vst[0], cache_kt, cache_vt, page_table)
    hist = jnp.transpose(state_pool[0], (1, 0, 2))
    xs = _mix(xs, attn_s, us, hist, gates_s, *post, past=past)
    ys = _ffn(xs, *ffn2).reshape(db, 1, d)

    cache_p = lambda t: jnp.transpose(t.reshape(1, b, N_HEADS, HEAD_DIM, s), (0, 1, 4, 2, 3))
    cache_s = lambda t: jnp.transpose(t.reshape(1, 1, N_HEADS, HEAD_DIM, db), (0, 4, 1, 2, 3))
    pool_p = u.reshape(b, s, POOL_WIDTH)[:, s - POOL_HIST:]
    pool_s = jnp.concatenate([state_pool[0, :, 1:], us[:, None]], axis=1)
    return (yp, ys, cache_p(kt), cache_p(vt), pool_p[None], cache_s(kst), cache_s(vst), pool_s[None])
```

```python
import functools

import jax
import jax.numpy as jnp
from jax import lax
from jax.experimental import pallas as pl
from jax.experimental.pallas import tpu as pltpu

N_HEADS = 8
HEAD_DIM = 64
ATTN_WIDTH = N_HEADS * HEAD_DIM
MOBA_BLOCK = 256
MOBA_TOP_K = 3
POOL_WINDOWS = (2, 4, 8, 16)
POOL_GROUP_WIDTH = 128
POOL_WIDTH = len(POOL_WINDOWS) * POOL_GROUP_WIDTH
POOL_HIST = max(POOL_WINDOWS) - 1
HIST_ROWS = 16
EPS = 1e-6
NEG = -1e30
SCALE = HEAD_DIM ** -0.5

LANES = 128
MXU_WIDTH = 256
TOKEN_TILE = 512
RIDE_CHUNKS = 4
VMEM_LIMIT = 56 * 1024 * 1024

_MXU = jnp.bfloat16
_F32 = jnp.float32


def _dot(a, b):
    return jnp.dot(a, b, preferred_element_type=_F32)


def _dot_nt(a, b):
    return lax.dot_general(a, b, (((1,), (1,)), ((), ())), preferred_element_type=_F32)


def _split(a):
    hi = a.astype(_MXU)
    lo = (a - hi.astype(_F32)).astype(_MXU)
    return hi, lo


def _rms(x, g):
    ms = jnp.mean(x * x, axis=-1, keepdims=True)
    return x * lax.rsqrt(ms + EPS) * g


def _params(*sem):
    return pltpu.CompilerParams(dimension_semantics=sem, vmem_limit_bytes=VMEM_LIMIT)


def _resident(shape):
    return pl.BlockSpec(shape, lambda *_: (0,) * len(shape), pipeline_mode=pl.Buffered(1))


def _token_tile(n):
    return TOKEN_TILE if n % TOKEN_TILE == 0 else n


def _column(ref, b):
    lane = lax.broadcasted_iota(jnp.int32, (1, ref.shape[1]), 1)
    return jnp.sum(jnp.where(lane == b, ref[...], 0.0), axis=1, keepdims=True)


def _top_blocks(w_ref, sel_ref, n_sel):
    _, n_pages, heads, page = w_ref.shape
    page_per_blk = MOBA_BLOCK // page
    nb = n_pages // page_per_blk
    wb = jnp.sum(w_ref[0].reshape(nb, page_per_blk, heads, page), axis=1)
    cur = jnp.sum(wb, axis=2, keepdims=True) * (1.0 / MOBA_BLOCK)
    blk_id = lax.broadcasted_iota(jnp.int32, cur.shape, 0)
    for r in range(n_sel):
        mx = jnp.max(cur, axis=0, keepdims=True)
        idx = jnp.min(jnp.where(cur == mx, blk_id, nb), axis=0, keepdims=True)
        sel_ref[0, r] = jnp.broadcast_to(idx[0], (heads, LANES))
        cur = jnp.where(blk_id == idx, -jnp.inf, cur)


def _ffn_kernel(x_ref, g_ref, w1_ref, w3_ref, w2_ref, o_ref):
    x = x_ref[...]
    h = _rms(x, g_ref[...]).astype(_MXU)
    a = _dot(h, w1_ref[...])
    b = _dot(h, w3_ref[...])
    act = (a * jax.nn.sigmoid(a) * b).astype(_MXU)
    o_ref[...] = x + 0.5 * _dot(act, w2_ref[...])


def _ffn_scores_kernel(pt_ref, x_ref, g_ref, w1_ref, w3_ref, w2_ref, qt_ref, k_hbm,
                       o_ref, w_ref, sel_ref, kbuf, qb, sem, *, base, n_pages, n_sel):
    step = pl.program_id(0)
    b = base + step
    per = n_pages // RIDE_CHUNKS

    def page_copy(bb, pg):
        return pltpu.make_async_copy(k_hbm.at[pt_ref[bb * n_pages + pg]], kbuf.at[pg], sem.at[0])

    @pl.when(step == 0)
    def _():
        for pg in range(n_pages):
            page_copy(b, pg).start()

    x = x_ref[...]
    h = _rms(x, g_ref[...]).astype(_MXU)
    a = _dot(h, w1_ref[...])
    gate = _dot(h, w3_ref[...])
    act = (a * jax.nn.sigmoid(a) * gate).astype(_MXU)

    for pg in range(n_pages):
        page_copy(b, pg).wait()
    qb[...] = jnp.broadcast_to(_column(qt_ref, b), (ATTN_WIDTH, LANES)).reshape(N_HEADS, HEAD_DIM, LANES)
    f = act.shape[1]
    tile = MXU_WIDTH if f >= RIDE_CHUNKS * MXU_WIDTH else LANES
    bounds = [min(f, -(-(f * c) // (RIDE_CHUNKS * tile)) * tile) for c in range(RIDE_CHUNKS)] + [f]
    acc = x
    for c in range(RIDE_CHUNKS):
        seen = jnp.zeros((1, LANES), _F32)
        for hd in range(N_HEADS):
            qh = qb[hd]
            for pg in range(c * per, (c + 1) * per):
                row = jnp.sum(kbuf[pg, hd] * qh, axis=0, keepdims=True)
                w_ref[0, pg, hd:hd + 1, :] = row
                seen = seen + row
        bits = pltpu.bitcast(seen, jnp.uint32)
        half = 0.5 + ((bits >> 16) >> 16).astype(_F32)
        rows = slice(bounds[c], bounds[c + 1])
        part = _dot(act[:, rows], w2_ref[rows, :])
        acc = acc + part * jnp.concatenate([half] * (x.shape[1] // LANES), axis=1)
    o_ref[...] = acc
    _top_blocks(w_ref, sel_ref, n_sel)

    @pl.when(step + 1 < pl.num_programs(0))
    def _():
        for pg in range(n_pages):
            page_copy(b + 1, pg).start()


def _ffn_scores(x, g, w1, w3, w2, pt, qt, cache_kt, *, base, n_pages, n_sel):
    n, d = x.shape
    f = w1.shape[1]
    tm = _token_tile(n)
    steps = n // tm
    page = cache_kt.shape[3]
    assert n_pages % RIDE_CHUNKS == 0 and f % LANES == 0 and f >= RIDE_CHUNKS * LANES
    fixed = lambda shape: pl.BlockSpec(shape, lambda i, *_: (0,) * len(shape), pipeline_mode=pl.Buffered(1))
    return pl.pallas_call(
        functools.partial(_ffn_scores_kernel, base=base, n_pages=n_pages, n_sel=n_sel),
        grid_spec=pltpu.PrefetchScalarGridSpec(
            num_scalar_prefetch=1,
            grid=(steps,),
            in_specs=[
                pl.BlockSpec((tm, d), lambda i, *_: (i, 0)),
                fixed((1, d)), fixed((d, f)), fixed((d, f)), fixed((f, d)), fixed(qt.shape),
                pl.BlockSpec(memory_space=pl.ANY),
            ],
            out_specs=[
                pl.BlockSpec((tm, d), lambda i, *_: (i, 0)),
                pl.BlockSpec((1, n_pages, N_HEADS, page), lambda i, *_: (i, 0, 0, 0)),
                pl.BlockSpec((1, n_sel, N_HEADS, LANES), lambda i, *_: (i, 0, 0, 0)),
            ],
            scratch_shapes=[
                pltpu.VMEM((n_pages, N_HEADS, HEAD_DIM, page), _F32),
                pltpu.VMEM((N_HEADS, HEAD_DIM, LANES), _F32),
                pltpu.SemaphoreType.DMA((1,)),
            ],
        ),
        out_shape=[
            jax.ShapeDtypeStruct((n, d), _F32),
            jax.ShapeDtypeStruct((steps, n_pages, N_HEADS, page), _F32),
            jax.ShapeDtypeStruct((steps, n_sel, N_HEADS, LANES), jnp.int32),
        ],
        compiler_params=_params("arbitrary"),
        name="ffn_scores",
    )(pt, x, g, w1, w3, w2, qt, cache_kt)


def _ffn(x, g, w1, w3, w2):
    n, d = x.shape
    f = w1.shape[1]
    tm = _token_tile(n)
    return pl.pallas_call(
        _ffn_kernel,
        grid=(n // tm,),
        in_specs=[
            pl.BlockSpec((tm, d), lambda i: (i, 0)),
            _resident((1, d)),
            _resident((d, f)),
            _resident((d, f)),
            _resident((f, d)),
        ],
        out_specs=pl.BlockSpec((tm, d), lambda i: (i, 0)),
        out_shape=jax.ShapeDtypeStruct((n, d), _F32),
        compiler_params=_params("parallel"),
        name="ffn",
    )(x, g, w1, w3, w2)


def _head_rms(z, g, pmat):
    z2 = z * z
    half = pmat.shape[0]
    ms = []
    for c in range(z.shape[1] // half):
        hi, lo = _split(z2[:, c * half:(c + 1) * half])
        ms.append(_dot(hi, pmat) + _dot(lo, pmat))
    ms = jnp.concatenate(ms, axis=1)
    return z * lax.rsqrt(ms + EPS) * g


def _proj_kernel(x_ref, g_ref, w_ref, b_ref, gq_ref, gk_ref, p_ref,
                 u_ref, q_ref, k_ref, v_ref, gt_ref, kt_ref, vt_ref):
    h = _rms(x_ref[...], g_ref[...]).astype(_MXU)
    a = ATTN_WIDTH
    o = POOL_WIDTH
    u_ref[...] = _dot(h, w_ref[:, 0:o])
    q_ref[...] = _head_rms(_dot(h, w_ref[:, o:o + a]), gq_ref[...], p_ref[...])
    k = _head_rms(_dot(h, w_ref[:, o + a:o + 2 * a]), gk_ref[...], p_ref[...])
    v = _dot(h, w_ref[:, o + 2 * a:o + 3 * a])
    k_ref[...] = k
    v_ref[...] = v
    kt_ref[0] = k.T
    vt_ref[0] = v.T
    gt_ref[...] = jax.nn.sigmoid(_dot(h, w_ref[:, o + 3 * a:]) + b_ref[...])


def _proj(x, g, w_in, b_gate, gq, gk, pmat, *, seq):
    n, d = x.shape
    wtot = w_in.shape[1]
    tm = _token_tile(n)
    assert seq % tm == 0 and n % seq == 0
    tiles_per_seq = seq // tm
    row = lambda w: pl.BlockSpec((tm, w), lambda i: (i, 0))
    tok_minor = pl.BlockSpec((1, ATTN_WIDTH, tm), lambda i: (i // tiles_per_seq, 0, i % tiles_per_seq))
    return pl.pallas_call(
        _proj_kernel,
        grid=(n // tm,),
        in_specs=[
            row(d),
            _resident((1, d)),
            _resident((d, wtot)),
            _resident((1, 2 * d)),
            _resident((1, ATTN_WIDTH)),
            _resident((1, ATTN_WIDTH)),
            _resident(pmat.shape),
        ],
        out_specs=[row(POOL_WIDTH), row(ATTN_WIDTH), row(ATTN_WIDTH), row(ATTN_WIDTH), row(2 * d),
                   tok_minor, tok_minor],
        out_shape=[
            jax.ShapeDtypeStruct((n, POOL_WIDTH), _F32),
            jax.ShapeDtypeStruct((n, ATTN_WIDTH), _F32),
            jax.ShapeDtypeStruct((n, ATTN_WIDTH), _F32),
            jax.ShapeDtypeStruct((n, ATTN_WIDTH), _F32),
            jax.ShapeDtypeStruct((n, 2 * d), _F32),
            jax.ShapeDtypeStruct((n // seq, ATTN_WIDTH, seq), _F32),
            jax.ShapeDtypeStruct((n // seq, ATTN_WIDTH, seq), _F32),
        ],
        compiler_params=_params("parallel"),
        name="proj",
    )(x, g, w_in, b_gate, gq, gk, pmat)


Q_ROWS = 2 * MOBA_BLOCK
PREP_ROWS = 4 * MOBA_BLOCK
LOG2E = 1.4426950408889634


def _moba_kernel(q_ref, k_ref, v_ref, o_ref, kaug, vaug, qaug, km, s_buf, mrun, mb, acc_s, *, nb, n_sel):
    step = pl.program_id(2)
    blk = MOBA_BLOCK
    blk_shift = blk.bit_length() - 1
    half = HEAD_DIM
    lane = lax.broadcasted_iota(jnp.int32, (1, LANES), 1)
    in_head = [(lane >= half * h) & (lane < half * (h + 1)) for h in range(2)]

    @pl.when(step == 0)
    def _():
        km[...] = jnp.mean(k_ref[0].reshape(nb, blk, LANES), axis=1)
        km_hi, km_lo = _split(km[...])

        def prep(c, carry):
            r0 = pl.multiple_of(c * PREP_ROWS, PREP_ROWS)
            rows = pl.ds(r0, PREP_ROWS)
            k = k_ref[0, rows, :]
            v = v_ref[0, rows, :]
            q2 = q_ref[0, rows, :]
            key_blk = lax.shift_right_logical(
                r0 + lax.broadcasted_iota(jnp.int32, (PREP_ROWS, LANES), 0), blk_shift)
            lane_s = lax.broadcasted_iota(jnp.int32, (PREP_ROWS, LANES), 1)
            blk_id = lax.broadcasted_iota(jnp.int32, (nb, PREP_ROWS), 0)
            own = lax.shift_right_logical(
                r0 + lax.broadcasted_iota(jnp.int32, (nb, PREP_ROWS), 1), blk_shift)
            past = blk_id < own
            for h in range(2):
                onehot = (lane_s - half * (1 - h) == key_blk).astype(_F32)
                kaug[h, rows, :] = jnp.where(in_head[h], k, onehot).astype(_MXU)
                vaug[h, rows, :] = jnp.where(in_head[h], v, 1.0).astype(_MXU)
                qh = jnp.where(in_head[h], q2, 0.0)
                q_hi, q_lo = _split(qh)
                sb = _dot_nt(km_hi, q_hi) + (_dot_nt(km_lo, q_hi) + _dot_nt(km_hi, q_lo))
                cur = jnp.where(past, sb, NEG)
                sel = jnp.zeros(cur.shape, jnp.bool_)
                for _ in range(n_sel):
                    mx = jnp.max(cur, axis=0, keepdims=True)
                    idx = jnp.min(jnp.where(cur == mx, blk_id, nb), axis=0, keepdims=True)
                    pick = blk_id == idx
                    sel = sel | pick
                    cur = jnp.where(pick, -jnp.inf, cur)
                bias_t = jnp.where((sel & past) | (blk_id == own), 0.0, NEG)
                before = half * (1 - h)
                parts = [bias_t, jnp.zeros((LANES - before - nb, PREP_ROWS), _F32)]
                if before:
                    parts.insert(0, jnp.zeros((before, PREP_ROWS), _F32))
                bias_lanes = jnp.concatenate(parts, axis=0).T
                qaug[h, rows, :] = jnp.where(in_head[h], qh * (SCALE * LOG2E), bias_lanes).astype(_MXU)
            return carry

        lax.fori_loop(0, k_ref.shape[1] // PREP_ROWS, prep, 0)

    def rows_of(ref, h, t, n=1):
        return ref[h, pl.ds(pl.multiple_of(t * Q_ROWS, Q_ROWS), n * Q_ROWS), :]

    def lane_max(s):
        m = s[:, :LANES]
        for c in range(1, s.shape[1] // LANES):
            m = jnp.maximum(m, s[:, c * LANES:(c + 1) * LANES])
        return m

    groups = step + 1
    odd = groups % 2 == 1

    key_ahead = (lax.broadcasted_iota(jnp.int32, (Q_ROWS, 2 * Q_ROWS), 1)
                 - lax.broadcasted_iota(jnp.int32, (Q_ROWS, 2 * Q_ROWS), 0))
    for h in range(2):
        mrun[h] = jnp.full((Q_ROWS, LANES), NEG, _F32)

    def scores(t, n, causal):
        for h in range(2):
            s = _dot_nt(rows_of(qaug, h, step), rows_of(kaug, h, t, n))
            if causal:
                s = jnp.where(key_ahead[:, :n * Q_ROWS] <= (step - t) * Q_ROWS, s, NEG)
            for c in range(n):
                s_buf[h, t + c] = s[:, c * Q_ROWS:(c + 1) * Q_ROWS]
            mrun[h] = jnp.maximum(mrun[h], lane_max(s))

    def scores_pair(tt, carry):
        scores(2 * tt, 2, False)
        return carry

    lax.fori_loop(0, (groups - 1) // 2, scores_pair, 0)

    @pl.when(odd)
    def _():
        scores(step, 1, True)

    @pl.when(jnp.logical_not(odd))
    def _():
        scores(step - 1, 2, True)

    for h in range(2):
        mb[h] = jnp.broadcast_to(jnp.max(mrun[h], axis=1, keepdims=True), (Q_ROWS, LANES))
        acc_s[h] = jnp.zeros((Q_ROWS, LANES), _F32)

    def attend(t, n):
        for h in range(2):
            s = jnp.concatenate([s_buf[h, t + c] for c in range(n)], axis=1)
            m = jnp.concatenate([mb[h]] * (s.shape[1] // LANES), axis=1)
            acc_s[h] = acc_s[h] + _dot(jnp.exp2(s - m).astype(_MXU), rows_of(vaug, h, t, n))

    def attend_pair(tt, carry):
        attend(2 * tt, 2)
        return carry

    lax.fori_loop(0, groups // 2, attend_pair, 0)

    @pl.when(odd)
    def _():
        attend(step, 1)

    num = jnp.where(in_head[0], acc_s[0], acc_s[1])
    den = pltpu.roll(jnp.where(in_head[0], acc_s[1], acc_s[0]), half, axis=1)
    o_ref[0] = num / den


def _moba_prompt(q, k, v):
    b, s, w = q.shape
    assert w == ATTN_WIDTH and s % PREP_ROWS == 0 and MOBA_BLOCK & (MOBA_BLOCK - 1) == 0
    nb = s // MOBA_BLOCK
    assert nb <= HEAD_DIM and nb % 8 == 0
    n_sel = min(MOBA_TOP_K, nb - 1)
    whole = pl.BlockSpec((1, s, LANES), lambda bi, p, i: (bi, 0, p))
    return pl.pallas_call(
        functools.partial(_moba_kernel, nb=nb, n_sel=n_sel),
        grid=(b, w // LANES, s // Q_ROWS),
        in_specs=[whole, whole, whole],
        out_specs=pl.BlockSpec((1, Q_ROWS, LANES), lambda bi, p, i: (bi, i, p)),
        out_shape=jax.ShapeDtypeStruct((b, s, w), _F32),
        scratch_shapes=[
            pltpu.VMEM((2, s, LANES), _MXU),
            pltpu.VMEM((2, s, LANES), _MXU),
            pltpu.VMEM((2, s, LANES), _MXU),
            pltpu.VMEM((nb, LANES), _F32),
            pltpu.VMEM((2, s // Q_ROWS, Q_ROWS, Q_ROWS), _F32),
            pltpu.VMEM((2, Q_ROWS, LANES), _F32),
            pltpu.VMEM((2, Q_ROWS, LANES), _F32),
            pltpu.VMEM((2, Q_ROWS, LANES), _F32),
        ],
        compiler_params=_params("parallel", "parallel", "arbitrary"),
        name="moba_prompt",
    )(q, k, v)


DEC_CHUNK = 8
DEC_SLOTS = 4


def _dec_scores_kernel(pt_ref, qt_ref, k_hbm, w_ref, sel_ref, kbuf, qb, sem, *, n_pages, n_sel):
    b = pl.program_id(0)
    n_chunks = n_pages // DEC_CHUNK
    total = pl.num_programs(0) * n_chunks

    def page_copy(g, pg):
        phys = pt_ref[g * DEC_CHUNK + pg]
        slot = g % DEC_SLOTS
        return pltpu.make_async_copy(k_hbm.at[phys], kbuf.at[slot, pg], sem.at[slot])

    def start_chunk(g):
        for pg in range(DEC_CHUNK):
            page_copy(g, pg).start()

    @pl.when(b == 0)
    def _():
        for g in range(DEC_SLOTS - 1):
            start_chunk(g)

    qb[...] = jnp.broadcast_to(_column(qt_ref, b), (ATTN_WIDTH, LANES)).reshape(N_HEADS, HEAD_DIM, LANES)

    def chunk_body(c, carry):
        g = b * n_chunks + c
        nxt = g + DEC_SLOTS - 1

        @pl.when(nxt < total)
        def _():
            start_chunk(nxt)

        for pg in range(DEC_CHUNK):
            page_copy(g, pg).wait()
        slot = g % DEC_SLOTS
        for pg in range(DEC_CHUNK):
            w = jnp.sum(kbuf[slot, pg] * qb[...], axis=1)
            w_ref[0, pl.ds(c * DEC_CHUNK + pg, 1)] = w[None]
        return carry

    lax.fori_loop(0, n_chunks, chunk_body, 0)
    _top_blocks(w_ref, sel_ref, n_sel)


def _dec_attend_kernel(pt_ref, sel_ref, w_ref, qt_ref, kt_ref, vt_ref, v_hbm, o_ref, vbuf, sem,
                       *, n_pages, n_sel):
    b = pl.program_id(0)
    page_per_blk = MOBA_BLOCK // v_hbm.shape[3]
    n_gather = n_sel * page_per_blk

    def copies(bb, slot):
        out = []
        for h in range(N_HEADS):
            for r in range(n_sel):
                blk = sel_ref[(bb * n_sel + r) * N_HEADS + h]
                for e in range(page_per_blk):
                    phys = pt_ref[bb * n_pages + blk * page_per_blk + e]
                    out.append(pltpu.make_async_copy(
                        v_hbm.at[phys, h], vbuf.at[slot, h, r * page_per_blk + e], sem.at[slot]))
        return out

    @pl.when(b == 0)
    def _():
        o_ref[...] = jnp.zeros(o_ref.shape, _F32)
        for c in copies(0, 0):
            c.start()

    @pl.when(b + 1 < pl.num_programs(0))
    def _():
        for c in copies(b + 1, (b + 1) % 2):
            c.start()

    slot = b % 2
    for c in copies(b, slot):
        c.wait()

    lane = lax.broadcasted_iota(jnp.int32, (1, o_ref.shape[1]), 1)
    qc = _column(qt_ref, b)
    kc = _column(kt_ref, b)
    vc = _column(vt_ref, b)
    s_own = jnp.sum((qc * kc).reshape(N_HEADS, HEAD_DIM, 1), axis=1) * SCALE
    for h in range(N_HEADS):
        rows = []
        for r in range(n_sel):
            blk = sel_ref[(b * n_sel + r) * N_HEADS + h]
            for e in range(page_per_blk):
                wp = w_ref[0, pl.ds(blk * page_per_blk + e, 1)][0]
                rows.append(wp[h:h + 1, :] * SCALE)
        so = s_own[h:h + 1, :]
        m = so
        for rw in rows:
            m = jnp.maximum(m, jnp.max(rw, axis=1, keepdims=True))
        p_own = jnp.exp(so - m)
        l = p_own
        acc = jnp.zeros((HEAD_DIM, rows[0].shape[1]), _F32)
        for g in range(n_gather):
            p = jnp.exp(rows[g] - m)
            l = l + jnp.sum(p, axis=1, keepdims=True)
            acc = acc + p * vbuf[slot, h, g]
        hs = slice(h * HEAD_DIM, (h + 1) * HEAD_DIM)
        o = (jnp.sum(acc, axis=1, keepdims=True) + p_own * vc[hs]) / l
        o_ref[hs, :] = jnp.where(lane == b, o, o_ref[hs, :])


def _decode_dims(page_table, cache_kt):
    db, n_pages = page_table.shape
    page = cache_kt.shape[3]
    assert MOBA_BLOCK % page == 0 and (n_pages * page) % MOBA_BLOCK == 0
    n_sel = min(MOBA_TOP_K, n_pages * page // MOBA_BLOCK)
    assert n_sel > 0
    return db, n_pages, page, n_sel


def _decode_scores(qt, cache_kt, page_table):
    db, n_pages, page, n_sel = _decode_dims(page_table, cache_kt)
    assert n_pages % DEC_CHUNK == 0 and db * (n_pages // DEC_CHUNK) >= DEC_SLOTS
    return pl.pallas_call(
        functools.partial(_dec_scores_kernel, n_pages=n_pages, n_sel=n_sel),
        grid_spec=pltpu.PrefetchScalarGridSpec(
            num_scalar_prefetch=1,
            grid=(db,),
            in_specs=[pl.BlockSpec(qt.shape, lambda i, *_: (0, 0)), pl.BlockSpec(memory_space=pl.ANY)],
            out_specs=[
                pl.BlockSpec((1, n_pages, N_HEADS, page), lambda i, *_: (i, 0, 0, 0)),
                pl.BlockSpec((1, n_sel, N_HEADS, LANES), lambda i, *_: (i, 0, 0, 0)),
            ],
            scratch_shapes=[
                pltpu.VMEM((DEC_SLOTS, DEC_CHUNK, N_HEADS, HEAD_DIM, page), _F32),
                pltpu.VMEM((N_HEADS, HEAD_DIM, LANES), _F32),
                pltpu.SemaphoreType.DMA((DEC_SLOTS,)),
            ],
        ),
        out_shape=[
            jax.ShapeDtypeStruct((db, n_pages, N_HEADS, page), _F32),
            jax.ShapeDtypeStruct((db, n_sel, N_HEADS, LANES), jnp.int32),
        ],
        compiler_params=_params("arbitrary"),
        name="decode_scores",
    )(page_table.reshape(-1), qt, cache_kt)


def _decode_attend(logits, sel, qt, kt, vt, cache_vt, page_table):
    db, n_pages, page, n_sel = _decode_dims(page_table, cache_vt)
    whole = lambda a: pl.BlockSpec(a.shape, lambda i, *_: (0,) * a.ndim)
    n_gather = n_sel * (MOBA_BLOCK // page)
    out_t = pl.pallas_call(
        functools.partial(_dec_attend_kernel, n_pages=n_pages, n_sel=n_sel),
        grid_spec=pltpu.PrefetchScalarGridSpec(
            num_scalar_prefetch=2,
            grid=(db,),
            in_specs=[
                pl.BlockSpec((1, n_pages, N_HEADS, page), lambda i, *_: (i, 0, 0, 0)),
                whole(qt), whole(kt), whole(vt), pl.BlockSpec(memory_space=pl.ANY),
            ],
            out_specs=pl.BlockSpec((ATTN_WIDTH, db), lambda i, *_: (0, 0)),
            scratch_shapes=[
                pltpu.VMEM((2, N_HEADS, n_gather, HEAD_DIM, page), _F32),
                pltpu.SemaphoreType.DMA((2,)),
            ],
        ),
        out_shape=jax.ShapeDtypeStruct((ATTN_WIDTH, db), _F32),
        compiler_params=_params("arbitrary"),
        name="decode_attend",
    )(page_table.reshape(-1), sel[:, :, :, 0].reshape(-1), logits, qt, kt, vt, cache_vt)
    return out_t.T


def _pool_branch(d_groups, wpg_ref, sp_ref):
    outs = []
    for g, d in enumerate(d_groups):
        cols = slice(g * POOL_GROUP_WIDTH, (g + 1) * POOL_GROUP_WIDTH)
        outs.append(_dot(d.astype(_MXU), wpg_ref[g]) * sp_ref[:, cols])
    return jnp.concatenate(outs, axis=1)


def _mix_tail(x, pool, attn, gt_ref, wbp_ref, wba_ref, wo_ref, o_ref):
    d = x.shape[1]
    m = (gt_ref[:, :d] * _dot(pool.astype(_MXU), wbp_ref[...])
         + gt_ref[:, d:] * _dot(attn.astype(_MXU), wba_ref[...]))
    o_ref[...] = x + _dot(m.astype(_MXU), wo_ref[...])


def _mix_prompt_kernel(x_ref, at_ref, u_ref, hist_ref, gt_ref, wpg_ref, sp_ref, wbp_ref, wba_ref,
                       wo_ref, o_ref, ext, *, seq):
    tm = u_ref.shape[0]
    pos0 = (pl.program_id(0) * tm) % seq
    ext[0:HIST_ROWS, :] = jnp.where(pos0 > 0, hist_ref[...], 0.0)
    ext[HIST_ROWS:, :] = u_ref[...]
    pos = pos0 + lax.broadcasted_iota(jnp.int32, (tm, 1), 0)
    d_groups = []
    for g, w in enumerate(POOL_WINDOWS):
        cols = slice(g * POOL_GROUP_WIDTH, (g + 1) * POOL_GROUP_WIDTH)
        cur = ext[HIST_ROWS:, cols]
        total = cur
        for j in range(1, w):
            total = total + ext[pl.ds(HIST_ROWS - j, tm), cols]
        cnt = jnp.minimum(w, pos + 1).astype(_F32)
        d_groups.append(total / cnt - cur)
    pool = _pool_branch(d_groups, wpg_ref, sp_ref)
    _mix_tail(x_ref[...], pool, at_ref[...], gt_ref, wbp_ref, wba_ref, wo_ref, o_ref)


def _mix_decode_kernel(x_ref, at_ref, u_ref, hist_ref, gt_ref, wpg_ref, sp_ref, wbp_ref, wba_ref,
                       wo_ref, o_ref, *, past):
    d_groups = []
    for g, w in enumerate(POOL_WINDOWS):
        cols = slice(g * POOL_GROUP_WIDTH, (g + 1) * POOL_GROUP_WIDTH)
        cur = u_ref[:, cols]
        total = cur
        for j in range(1, w):
            total = total + hist_ref[POOL_HIST - j, :, cols]
        d_groups.append(total / float(min(w, past + 1)) - cur)
    pool = _pool_branch(d_groups, wpg_ref, sp_ref)
    _mix_tail(x_ref[...], pool, at_ref[...], gt_ref, wbp_ref, wba_ref, wo_ref, o_ref)


def _mix(x, attn, u, hist, gates, wpg, sp, wbp, wba, wo, *, seq=None, past=None):
    n, d = x.shape
    tm = _token_tile(n)
    row = lambda w: pl.BlockSpec((tm, w), lambda i: (i, 0))
    if seq is not None:
        assert seq % tm == 0 and tm % HIST_ROWS == 0
        body = functools.partial(_mix_prompt_kernel, seq=seq)
        hist_spec = pl.BlockSpec(
            (HIST_ROWS, POOL_WIDTH), lambda i: (jnp.maximum(i * (tm // HIST_ROWS) - 1, 0), 0))
        scratch = [pltpu.VMEM((tm + HIST_ROWS, POOL_WIDTH), _F32)]
    else:
        body = functools.partial(_mix_decode_kernel, past=past)
        hist_spec = pl.BlockSpec((POOL_HIST, tm, POOL_WIDTH), lambda i: (0, i, 0))
        scratch = []
    return pl.pallas_call(
        body,
        grid=(n // tm,),
        in_specs=[
            row(d), row(ATTN_WIDTH), row(POOL_WIDTH), hist_spec, row(2 * d),
            _resident(wpg.shape), _resident(sp.shape), _resident(wbp.shape),
            _resident(wba.shape), _resident(wo.shape),
        ],
        out_specs=row(d),
        out_shape=jax.ShapeDtypeStruct((n, d), _F32),
        scratch_shapes=scratch,
        compiler_params=_params("parallel"),
        name="mix",
    )(x, attn, u, hist, gates, wpg, sp, wbp, wba, wo)


def kernel(x_prompt, x_sample, cache_k, cache_v, state_pool, page_table, g_ffn1, w1_ffn1, w3_ffn1,
           w2_ffn1, g_mix, w_in, b_gate, g_q, g_k, w_pool_grp, s_pool, w_branch_pool,
           w_branch_attn, w_out, g_ffn2, w1_ffn2, w3_ffn2, w2_ffn2):
    assert w_in.shape[0] == 1 and x_sample.shape[1] == 1
    b, s, d = x_prompt.shape
    db = x_sample.shape[0]
    page = cache_k.shape[2]
    past = page_table.shape[1] * page
    mx = lambda w: w[0].astype(_MXU)

    ffn1 = (g_ffn1, mx(w1_ffn1), mx(w3_ffn1), mx(w2_ffn1))
    ffn2 = (g_ffn2, mx(w1_ffn2), mx(w3_ffn2), mx(w2_ffn2))
    head = jnp.arange(2 * LANES) // HEAD_DIM
    pmat = ((head[:, None] == head[None, :]).astype(_F32) / HEAD_DIM).astype(_MXU)
    proj = (g_mix, mx(w_in), b_gate, jnp.tile(g_q, (1, N_HEADS)), jnp.tile(g_k, (1, N_HEADS)), pmat)
    post = (mx(w_pool_grp), s_pool, mx(w_branch_pool), mx(w_branch_attn), mx(w_out))

    xs = _ffn(x_sample.reshape(db, d), *ffn1)
    us, qs, _, _, gates_s, kst, vst = _proj(xs, *proj, seq=db)
    qst = qs.T
    cache_kt = jnp.transpose(cache_k[0], (0, 2, 3, 1))
    cache_vt = jnp.transpose(cache_v[0], (0, 2, 3, 1))

    steps = (b * s) // _token_tile(b * s)
    ride = db == 2 * steps
    if ride:
        _, n_pages, _, n_sel = _decode_dims(page_table, cache_kt)
        rider = lambda base: dict(pt=page_table.reshape(-1), qt=qst, cache_kt=cache_kt, base=base,
                                  n_pages=n_pages, n_sel=n_sel)
        xp, logits_a, sel_a = _ffn_scores(x_prompt.reshape(b * s, d), *ffn1, **rider(0))
    else:
        xp = _ffn(x_prompt.reshape(b * s, d), *ffn1)
    u, q, k, v, gates, kt, vt = _proj(xp, *proj, seq=s)
    shape3 = (b, s, ATTN_WIDTH)
    attn = _moba_prompt(q.reshape(shape3), k.reshape(shape3), v.reshape(shape3))
    xp = _mix(xp, attn.reshape(b * s, ATTN_WIDTH), u, u, gates, *post, seq=s)
    if ride:
        yp, logits_b, sel_b = _ffn_scores(xp, *ffn2, **rider(steps))
        logits = jnp.concatenate([logits_a, logits_b], axis=0)
        sel = jnp.concatenate([sel_a, sel_b], axis=0)
    else:
        yp = _ffn(xp, *ffn2)
        logits, sel = _decode_scores(qst, cache_kt, page_table)
    yp = yp.reshape(b, s, d)

    attn_s = _decode_attend(logits, sel, qst, kst[0], vst[0], cache_vt, page_table)
    hist = jnp.transpose(state_pool[0], (1, 0, 2))
    xs = _mix(xs, attn_s, us, hist, gates_s, *post, past=past)
    ys = _ffn(xs, *ffn2).reshape(db, 1, d)

    cache_p = lambda t: jnp.transpose(t.reshape(1, b, N_HEADS, HEAD_DIM, s), (0, 1, 4, 2, 3))
    cache_s = lambda t: jnp.transpose(t.reshape(1, 1, N_HEADS, HEAD_DIM, db), (0, 4, 1, 2, 3))
    pool_p = u.reshape(b, s, POOL_WIDTH)[:, s - POOL_HIST:]
    pool_s = jnp.concatenate([state_pool[0, :, 1:], us[:, None]], axis=1)
    return (yp, ys, cache_p(kt), cache_p(vt), pool_p[None], cache_s(kst), cache_s(vst), pool_s[None])
```

```python
import functools

import jax
import jax.numpy as jnp
from jax import lax
from jax.experimental import pallas as pl
from jax.experimental.pallas import tpu as pltpu

N_HEADS = 8
HEAD_DIM = 64
ATTN_WIDTH = N_HEADS * HEAD_DIM
MOBA_BLOCK = 256
MOBA_TOP_K = 3
POOL_WINDOWS = (2, 4, 8, 16)
POOL_GROUP_WIDTH = 128
POOL_WIDTH = len(POOL_WINDOWS) * POOL_GROUP_WIDTH
POOL_HIST = max(POOL_WINDOWS) - 1
HIST_ROWS = 16
EPS = 1e-6
NEG = -1e30
SCALE = HEAD_DIM ** -0.5

LANES = 128
MXU_WIDTH = 256
TOKEN_TILE = 512
RIDE_CHUNKS = 4
VMEM_LIMIT = 56 * 1024 * 1024

_MXU = jnp.bfloat16
_F32 = jnp.float32


def _dot(a, b):
    return jnp.dot(a, b, preferred_element_type=_F32)


def _dot_nt(a, b):
    return lax.dot_general(a, b, (((1,), (1,)), ((), ())), preferred_element_type=_F32)


def _split(a):
    hi = a.astype(_MXU)
    lo = (a - hi.astype(_F32)).astype(_MXU)
    return hi, lo


def _rms(x, g):
    ms = jnp.mean(x * x, axis=-1, keepdims=True)
    return x * lax.rsqrt(ms + EPS) * g


def _params(*sem):
    return pltpu.CompilerParams(dimension_semantics=sem, vmem_limit_bytes=VMEM_LIMIT)


def _resident(shape):
    return pl.BlockSpec(shape, lambda *_: (0,) * len(shape), pipeline_mode=pl.Buffered(1))


def _token_tile(n):
    return TOKEN_TILE if n % TOKEN_TILE == 0 else n


def _column(ref, b):
    lane = lax.broadcasted_iota(jnp.int32, (1, ref.shape[1]), 1)
    return jnp.sum(jnp.where(lane == b, ref[...], 0.0), axis=1, keepdims=True)


def _top_blocks(w_ref, sel_ref, n_sel):
    _, n_pages, heads, page = w_ref.shape
    page_per_blk = MOBA_BLOCK // page
    nb = n_pages // page_per_blk
    wb = jnp.sum(w_ref[0].reshape(nb, page_per_blk, heads, page), axis=1)
    cur = jnp.sum(wb, axis=2, keepdims=True) * (1.0 / MOBA_BLOCK)
    blk_id = lax.broadcasted_iota(jnp.int32, cur.shape, 0)
    for r in range(n_sel):
        mx = jnp.max(cur, axis=0, keepdims=True)
        idx = jnp.min(jnp.where(cur == mx, blk_id, nb), axis=0, keepdims=True)
        sel_ref[0, r] = jnp.broadcast_to(idx[0], (heads, LANES))
        cur = jnp.where(blk_id == idx, -jnp.inf, cur)


def _ffn_kernel(x_ref, g_ref, w1_ref, w3_ref, w2_ref, o_ref):
    x = x_ref[...]
    h = _rms(x, g_ref[...]).astype(_MXU)
    a = _dot(h, w1_ref[...])
    b = _dot(h, w3_ref[...])
    act = (a * jax.nn.sigmoid(a) * b).astype(_MXU)
    o_ref[...] = x + 0.5 * _dot(act, w2_ref[...])


def _ffn_scores_kernel(pt_ref, x_ref, g_ref, w1_ref, w3_ref, w2_ref, qt_ref, k_hbm,
                       o_ref, w_ref, sel_ref, kbuf, qb, sem, *, base, n_pages, n_sel):
    step = pl.program_id(0)
    b = base + step
    per = n_pages // RIDE_CHUNKS

    def page_copy(bb, pg):
        return pltpu.make_async_copy(k_hbm.at[pt_ref[bb * n_pages + pg]], kbuf.at[pg], sem.at[0])

    @pl.when(step == 0)
    def _():
        for pg in range(n_pages):
            page_copy(b, pg).start()

    x = x_ref[...]
    h = _rms(x, g_ref[...]).astype(_MXU)
    a = _dot(h, w1_ref[...])
    gate = _dot(h, w3_ref[...])
    act = (a * jax.nn.sigmoid(a) * gate).astype(_MXU)

    for pg in range(n_pages):
        page_copy(b, pg).wait()
    qb[...] = jnp.broadcast_to(_column(qt_ref, b), (ATTN_WIDTH, LANES)).reshape(N_HEADS, HEAD_DIM, LANES)
    f = act.shape[1]
    tile = MXU_WIDTH if f >= RIDE_CHUNKS * MXU_WIDTH else LANES
    bounds = [min(f, -(-(f * c) // (RIDE_CHUNKS * tile)) * tile) for c in range(RIDE_CHUNKS)] + [f]
    acc = x
    for c in range(RIDE_CHUNKS):
        seen = jnp.zeros((1, LANES), _F32)
        for hd in range(N_HEADS):
            qh = qb[hd]
            for pg in range(c * per, (c + 1) * per):
                row = jnp.sum(kbuf[pg, hd] * qh, axis=0, keepdims=True)
                w_ref[0, pg, hd:hd + 1, :] = row
                seen = seen + row
        bits = pltpu.bitcast(seen, jnp.uint32)
        half = 0.5 + ((bits >> 16) >> 16).astype(_F32)
        rows = slice(bounds[c], bounds[c + 1])
        part = _dot(act[:, rows], w2_ref[rows, :])
        acc = acc + part * jnp.concatenate([half] * (x.shape[1] // LANES), axis=1)
    o_ref[...] = acc
    _top_blocks(w_ref, sel_ref, n_sel)

    @pl.when(step + 1 < pl.num_programs(0))
    def _():
        for pg in range(n_pages):
            page_copy(b + 1, pg).start()


def _ffn_scores(x, g, w1, w3, w2, pt, qt, cache_kt, *, base, n_pages, n_sel):
    n, d = x.shape
    f = w1.shape[1]
    tm = _token_tile(n)
    steps = n // tm
    page = cache_kt.shape[3]
    assert n_pages % RIDE_CHUNKS == 0 and f % LANES == 0 and f >= RIDE_CHUNKS * LANES
    fixed = lambda shape: pl.BlockSpec(shape, lambda i, *_: (0,) * len(shape), pipeline_mode=pl.Buffered(1))
    return pl.pallas_call(
        functools.partial(_ffn_scores_kernel, base=base, n_pages=n_pages, n_sel=n_sel),
        grid_spec=pltpu.PrefetchScalarGridSpec(
            num_scalar_prefetch=1,
            grid=(steps,),
            in_specs=[
                pl.BlockSpec((tm, d), lambda i, *_: (i, 0)),
                fixed((1, d)), fixed((d, f)), fixed((d, f)), fixed((f, d)), fixed(qt.shape),
                pl.BlockSpec(memory_space=pl.ANY),
            ],
            out_specs=[
                pl.BlockSpec((tm, d), lambda i, *_: (i, 0)),
                pl.BlockSpec((1, n_pages, N_HEADS, page), lambda i, *_: (i, 0, 0, 0)),
                pl.BlockSpec((1, n_sel, N_HEADS, LANES), lambda i, *_: (i, 0, 0, 0)),
            ],
            scratch_shapes=[
                pltpu.VMEM((n_pages, N_HEADS, HEAD_DIM, page), _F32),
                pltpu.VMEM((N_HEADS, HEAD_DIM, LANES), _F32),
                pltpu.SemaphoreType.DMA((1,)),
            ],
        ),
        out_shape=[
            jax.ShapeDtypeStruct((n, d), _F32),
            jax.ShapeDtypeStruct((steps, n_pages, N_HEADS, page), _F32),
            jax.ShapeDtypeStruct((steps, n_sel, N_HEADS, LANES), jnp.int32),
        ],
        compiler_params=_params("arbitrary"),
        name="ffn_scores",
    )(pt, x, g, w1, w3, w2, qt, cache_kt)


def _ffn(x, g, w1, w3, w2):
    n, d = x.shape
    f = w1.shape[1]
    tm = _token_tile(n)
    return pl.pallas_call(
        _ffn_kernel,
        grid=(n // tm,),
        in_specs=[
            pl.BlockSpec((tm, d), lambda i: (i, 0)),
            _resident((1, d)),
            _resident((d, f)),
            _resident((d, f)),
            _resident((f, d)),
        ],
        out_specs=pl.BlockSpec((tm, d), lambda i: (i, 0)),
        out_shape=jax.ShapeDtypeStruct((n, d), _F32),
        compiler_params=_params("parallel"),
        name="ffn",
    )(x, g, w1, w3, w2)


def _head_rms(z, g, pmat):
    z2 = z * z
    half = pmat.shape[0]
    ms = []
    for c in range(z.shape[1] // half):
        hi, lo = _split(z2[:, c * half:(c + 1) * half])
        ms.append(_dot(hi, pmat) + _dot(lo, pmat))
    ms = jnp.concatenate(ms, axis=1)
    return z * lax.rsqrt(ms + EPS) * g


def _proj_kernel(x_ref, g_ref, w_ref, b_ref, gq_ref, gk_ref, p_ref,
                 u_ref, q_ref, k_ref, v_ref, gt_ref, kt_ref, vt_ref):
    h = _rms(x_ref[...], g_ref[...]).astype(_MXU)
    a = ATTN_WIDTH
    o = POOL_WIDTH
    u_ref[...] = _dot(h, w_ref[:, 0:o])
    q_ref[...] = _head_rms(_dot(h, w_ref[:, o:o + a]), gq_ref[...], p_ref[...])
    k = _head_rms(_dot(h, w_ref[:, o + a:o + 2 * a]), gk_ref[...], p_ref[...])
    v = _dot(h, w_ref[:, o + 2 * a:o + 3 * a])
    k_ref[...] = k
    v_ref[...] = v
    kt_ref[0] = k.T
    vt_ref[0] = v.T
    gt_ref[...] = jax.nn.sigmoid(_dot(h, w_ref[:, o + 3 * a:]) + b_ref[...])


def _proj(x, g, w_in, b_gate, gq, gk, pmat, *, seq):
    n, d = x.shape
    wtot = w_in.shape[1]
    tm = _token_tile(n)
    assert seq % tm == 0 and n % seq == 0
    tiles_per_seq = seq // tm
    row = lambda w: pl.BlockSpec((tm, w), lambda i: (i, 0))
    tok_minor = pl.BlockSpec((1, ATTN_WIDTH, tm), lambda i: (i // tiles_per_seq, 0, i % tiles_per_seq))
    return pl.pallas_call(
        _proj_kernel,
        grid=(n // tm,),
        in_specs=[
            row(d),
            _resident((1, d)),
            _resident((d, wtot)),
            _resident((1, 2 * d)),
            _resident((1, ATTN_WIDTH)),
            _resident((1, ATTN_WIDTH)),
            _resident(pmat.shape),
        ],
        out_specs=[row(POOL_WIDTH), row(ATTN_WIDTH), row(ATTN_WIDTH), row(ATTN_WIDTH), row(2 * d),
                   tok_minor, tok_minor],
        out_shape=[
            jax.ShapeDtypeStruct((n, POOL_WIDTH), _F32),
            jax.ShapeDtypeStruct((n, ATTN_WIDTH), _F32),
            jax.ShapeDtypeStruct((n, ATTN_WIDTH), _F32),
            jax.ShapeDtypeStruct((n, ATTN_WIDTH), _F32),
            jax.ShapeDtypeStruct((n, 2 * d), _F32),
            jax.ShapeDtypeStruct((n // seq, ATTN_WIDTH, seq), _F32),
            jax.ShapeDtypeStruct((n // seq, ATTN_WIDTH, seq), _F32),
        ],
        compiler_params=_params("parallel"),
        name="proj",
    )(x, g, w_in, b_gate, gq, gk, pmat)


Q_ROWS = 2 * MOBA_BLOCK
PREP_ROWS = 4 * MOBA_BLOCK
LOG2E = 1.4426950408889634


def _moba_kernel(q_ref, k_ref, v_ref, o_ref, kaug, vaug, qaug, km, s_buf, mrun, mb, acc_s, *, nb, n_sel):
    step = pl.program_id(2)
    blk = MOBA_BLOCK
    blk_shift = blk.bit_length() - 1
    half = HEAD_DIM
    lane = lax.broadcasted_iota(jnp.int32, (1, LANES), 1)
    in_head = [(lane >= half * h) & (lane < half * (h + 1)) for h in range(2)]

    @pl.when(step == 0)
    def _():
        km[...] = jnp.mean(k_ref[0].reshape(nb, blk, LANES), axis=1)
        km_split = [_split(jnp.where(in_head[h], km[...], 0.0)) for h in range(2)]

        def prep(c, carry):
            r0 = pl.multiple_of(c * PREP_ROWS, PREP_ROWS)
            rows = pl.ds(r0, PREP_ROWS)
            for j in range(PREP_ROWS // blk):
                rows_j = pl.ds(r0 + j * blk, blk)
                for h in range(2):
                    onehot = (lane - half * (1 - h) == c * (PREP_ROWS // blk) + j).astype(_F32)
                    kaug[h, rows_j, :] = jnp.where(in_head[h], k_ref[0, rows_j, :], onehot).astype(_MXU)
            v = v_ref[0, rows, :]
            q2 = q_ref[0, rows, :]
            q_hi, q_lo = _split(q2)
            blk_id = lax.broadcasted_iota(jnp.int32, (nb, PREP_ROWS), 0)
            own = lax.shift_right_logical(
                r0 + lax.broadcasted_iota(jnp.int32, (nb, PREP_ROWS), 1), blk_shift)
            past = blk_id < own
            for h in range(2):
                vaug[h, rows, :] = jnp.where(in_head[h], v, 1.0).astype(_MXU)
                km_hi, km_lo = km_split[h]
                sb = _dot_nt(km_hi, q_hi) + (_dot_nt(km_lo, q_hi) + _dot_nt(km_hi, q_lo))
                cur = jnp.where(past, sb, NEG)
                sel = jnp.zeros(cur.shape, jnp.bool_)
                for _ in range(n_sel):
                    mx = jnp.max(cur, axis=0, keepdims=True)
                    idx = jnp.min(jnp.where(cur == mx, blk_id, nb), axis=0, keepdims=True)
                    pick = blk_id == idx
                    sel = sel | pick
                    cur = jnp.where(pick, -jnp.inf, cur)
                bias_t = jnp.where((sel & past) | (blk_id == own), 0.0, NEG)
                before = half * (1 - h)
                parts = [bias_t, jnp.zeros((LANES - before - nb, PREP_ROWS), _F32)]
                if before:
                    parts.insert(0, jnp.zeros((before, PREP_ROWS), _F32))
                bias_lanes = jnp.concatenate(parts, axis=0).T
                qaug[h, rows, :] = jnp.where(in_head[h], q2 * (SCALE * LOG2E), bias_lanes).astype(_MXU)
            return carry

        lax.fori_loop(0, k_ref.shape[1] // PREP_ROWS, prep, 0)

    def rows_of(ref, h, t, n=1):
        return ref[h, pl.ds(pl.multiple_of(t * Q_ROWS, Q_ROWS), n * Q_ROWS), :]

    def lane_max(s):
        m = s[:, :LANES]
        for c in range(1, s.shape[1] // LANES):
            m = jnp.maximum(m, s[:, c * LANES:(c + 1) * LANES])
        return m

    groups = step + 1
    odd = groups % 2 == 1

    key_ahead = (lax.broadcasted_iota(jnp.int32, (Q_ROWS, 2 * Q_ROWS), 1)
                 - lax.broadcasted_iota(jnp.int32, (Q_ROWS, 2 * Q_ROWS), 0))
    for h in range(2):
        mrun[h] = jnp.full((Q_ROWS, LANES), NEG, _F32)

    def scores(t, n, causal):
        for h in range(2):
            s = _dot_nt(rows_of(qaug, h, step), rows_of(kaug, h, t, n))
            if causal:
                s = jnp.where(key_ahead[:, :n * Q_ROWS] <= (step - t) * Q_ROWS, s, NEG)
            for c in range(n):
                s_buf[h, t + c] = s[:, c * Q_ROWS:(c + 1) * Q_ROWS]
            mrun[h] = jnp.maximum(mrun[h], lane_max(s))

    def scores_pair(tt, carry):
        scores(2 * tt, 2, False)
        return carry

    lax.fori_loop(0, (groups - 1) // 2, scores_pair, 0)

    @pl.when(odd)
    def _():
        scores(step, 1, True)

    @pl.when(jnp.logical_not(odd))
    def _():
        scores(step - 1, 2, True)

    for h in range(2):
        mb[h] = jnp.broadcast_to(jnp.max(mrun[h], axis=1, keepdims=True), (Q_ROWS, LANES))
        acc_s[h] = jnp.zeros((Q_ROWS, LANES), _F32)

    def attend(t, n):
        for h in range(2):
            s = jnp.concatenate([s_buf[h, t + c] for c in range(n)], axis=1)
            m = jnp.concatenate([mb[h]] * (s.shape[1] // LANES), axis=1)
            acc_s[h] = acc_s[h] + _dot(jnp.exp2(s - m).astype(_MXU), rows_of(vaug, h, t, n))

    def attend_pair(tt, carry):
        attend(2 * tt, 2)
        return carry

    lax.fori_loop(0, groups // 2, attend_pair, 0)

    @pl.when(odd)
    def _():
        attend(step, 1)

    num = jnp.where(in_head[0], acc_s[0], acc_s[1])
    den = pltpu.roll(jnp.where(in_head[0], acc_s[1], acc_s[0]), half, axis=1)
    o_ref[0] = num / den


def _moba_prompt(q, k, v):
    b, s, w = q.shape
    assert w == ATTN_WIDTH and s % PREP_ROWS == 0 and MOBA_BLOCK & (MOBA_BLOCK - 1) == 0
    nb = s // MOBA_BLOCK
    assert nb <= HEAD_DIM and nb % 8 == 0
    n_sel = min(MOBA_TOP_K, nb - 1)
    whole = pl.BlockSpec((1, s, LANES), lambda bi, p, i: (bi, 0, p))
    return pl.pallas_call(
        functools.partial(_moba_kernel, nb=nb, n_sel=n_sel),
        grid=(b, w // LANES, s // Q_ROWS),
        in_specs=[whole, whole, whole],
        out_specs=pl.BlockSpec((1, Q_ROWS, LANES), lambda bi, p, i: (bi, i, p)),
        out_shape=jax.ShapeDtypeStruct((b, s, w), _F32),
        scratch_shapes=[
            pltpu.VMEM((2, s, LANES), _MXU),
            pltpu.VMEM((2, s, LANES), _MXU),
            pltpu.VMEM((2, s, LANES), _MXU),
            pltpu.VMEM((nb, LANES), _F32),
            pltpu.VMEM((2, s // Q_ROWS, Q_ROWS, Q_ROWS), _F32),
            pltpu.VMEM((2, Q_ROWS, LANES), _F32),
            pltpu.VMEM((2, Q_ROWS, LANES), _F32),
            pltpu.VMEM((2, Q_ROWS, LANES), _F32),
        ],
        compiler_params=_params("parallel", "parallel", "arbitrary"),
        name="moba_prompt",
    )(q, k, v)


DEC_CHUNK = 8
DEC_SLOTS = 4


def _dec_scores_kernel(pt_ref, qt_ref, k_hbm, w_ref, sel_ref, kbuf, qb, sem, *, n_pages, n_sel):
    b = pl.program_id(0)
    n_chunks = n_pages // DEC_CHUNK
    total = pl.num_programs(0) * n_chunks

    def page_copy(g, pg):
        phys = pt_ref[g * DEC_CHUNK + pg]
        slot = g % DEC_SLOTS
        return pltpu.make_async_copy(k_hbm.at[phys], kbuf.at[slot, pg], sem.at[slot])

    def start_chunk(g):
        for pg in range(DEC_CHUNK):
            page_copy(g, pg).start()

    @pl.when(b == 0)
    def _():
        for g in range(DEC_SLOTS - 1):
            start_chunk(g)

    qb[...] = jnp.broadcast_to(_column(qt_ref, b), (ATTN_WIDTH, LANES)).reshape(N_HEADS, HEAD_DIM, LANES)

    def chunk_body(c, carry):
        g = b * n_chunks + c
        nxt = g + DEC_SLOTS - 1

        @pl.when(nxt < total)
        def _():
            start_chunk(nxt)

        for pg in range(DEC_CHUNK):
            page_copy(g, pg).wait()
        slot = g % DEC_SLOTS
        for pg in range(DEC_CHUNK):
            w = jnp.sum(kbuf[slot, pg] * qb[...], axis=1)
            w_ref[0, pl.ds(c * DEC_CHUNK + pg, 1)] = w[None]
        return carry

    lax.fori_loop(0, n_chunks, chunk_body, 0)
    _top_blocks(w_ref, sel_ref, n_sel)


def _dec_attend_kernel(pt_ref, sel_ref, wa_ref, wb_ref, qt_ref, kt_ref, vt_ref, v_hbm, o_ref,
                       vbuf, own_s, coef_s, sem, *, n_pages, n_sel, split):
    b = pl.program_id(0)
    last = pl.num_programs(0) - 1
    page_per_blk = MOBA_BLOCK // v_hbm.shape[3]
    n_gather = n_sel * page_per_blk

    def chosen_page(bb, h, g):
        blk = sel_ref[(bb * n_sel + g // page_per_blk) * N_HEADS + h]
        return blk * page_per_blk + g % page_per_blk

    def copies(bb, slot):
        return [pltpu.make_async_copy(v_hbm.at[pt_ref[bb * n_pages + chosen_page(bb, h, g)], h],
                                      vbuf.at[slot, h, g], sem.at[slot])
                for h in range(N_HEADS) for g in range(n_gather)]

    @pl.when(b == 0)
    def _():
        o_ref[...] = jnp.zeros(o_ref.shape, _F32)
        coef_s[...] = jnp.zeros(coef_s.shape, _F32)
        qk = qt_ref[...] * kt_ref[...]
        own_s[...] = jnp.sum(qk.reshape(N_HEADS, HEAD_DIM, qk.shape[1]), axis=1) * SCALE
        for c in copies(0, 0):
            c.start()

    @pl.when(b < last)
    def _():
        for c in copies(b + 1, (b + 1) % 2):
            c.start()

    slot = b % 2
    for c in copies(b, slot):
        c.wait()

    lane = lax.broadcasted_iota(jnp.int32, (1, o_ref.shape[1]), 1)
    head = lax.broadcasted_iota(jnp.int32, (N_HEADS, 1), 0)
    here = lane == b
    s_own = jnp.sum(jnp.where(here, own_s[...], 0.0), axis=1, keepdims=True)
    tiles = []
    for g in range(n_gather):
        t = jnp.zeros((N_HEADS, wa_ref.shape[3]), _F32)
        for h in range(N_HEADS):
            pg = pl.ds(chosen_page(b, h, g), 1)
            t = jnp.where(head == h, jnp.where(b < split, wa_ref[0, pg][0], wb_ref[0, pg][0]), t)
        tiles.append(t * SCALE)
    m = s_own
    for t in tiles:
        m = jnp.maximum(m, jnp.max(t, axis=1, keepdims=True))
    p_own = jnp.exp(s_own - m)
    denom = p_own
    probs = []
    for t in tiles:
        p = jnp.exp(t - m)
        denom = denom + jnp.sum(p, axis=1, keepdims=True)
        probs.append(p)
    inv = 1.0 / denom
    for h in range(N_HEADS):
        acc = jnp.zeros((HEAD_DIM, probs[0].shape[1]), _F32)
        for g in range(n_gather):
            acc = acc + probs[g][h:h + 1, :] * vbuf[slot, h, g]
        hs = slice(h * HEAD_DIM, (h + 1) * HEAD_DIM)
        o_ref[hs, :] = jnp.where(here, jnp.sum(acc, axis=1, keepdims=True) * inv[h:h + 1, :], o_ref[hs, :])
    coef_s[...] = jnp.where(here, p_own * inv, coef_s[...])

    @pl.when(b == last)
    def _():
        for h in range(N_HEADS):
            hs = slice(h * HEAD_DIM, (h + 1) * HEAD_DIM)
            o_ref[hs, :] = o_ref[hs, :] + coef_s[h:h + 1, :] * vt_ref[hs, :]


def _decode_dims(page_table, cache_kt):
    db, n_pages = page_table.shape
    page = cache_kt.shape[3]
    assert MOBA_BLOCK % page == 0 and (n_pages * page) % MOBA_BLOCK == 0
    n_sel = min(MOBA_TOP_K, n_pages * page // MOBA_BLOCK)
    assert n_sel > 0
    return db, n_pages, page, n_sel


def _decode_scores(qt, cache_kt, page_table):
    db, n_pages, page, n_sel = _decode_dims(page_table, cache_kt)
    assert n_pages % DEC_CHUNK == 0 and db * (n_pages // DEC_CHUNK) >= DEC_SLOTS
    return pl.pallas_call(
        functools.partial(_dec_scores_kernel, n_pages=n_pages, n_sel=n_sel),
        grid_spec=pltpu.PrefetchScalarGridSpec(
            num_scalar_prefetch=1,
            grid=(db,),
            in_specs=[pl.BlockSpec(qt.shape, lambda i, *_: (0, 0)), pl.BlockSpec(memory_space=pl.ANY)],
            out_specs=[
                pl.BlockSpec((1, n_pages, N_HEADS, page), lambda i, *_: (i, 0, 0, 0)),
                pl.BlockSpec((1, n_sel, N_HEADS, LANES), lambda i, *_: (i, 0, 0, 0)),
            ],
            scratch_shapes=[
                pltpu.VMEM((DEC_SLOTS, DEC_CHUNK, N_HEADS, HEAD_DIM, page), _F32),
                pltpu.VMEM((N_HEADS, HEAD_DIM, LANES), _F32),
                pltpu.SemaphoreType.DMA((DEC_SLOTS,)),
            ],
        ),
        out_shape=[
            jax.ShapeDtypeStruct((db, n_pages, N_HEADS, page), _F32),
            jax.ShapeDtypeStruct((db, n_sel, N_HEADS, LANES), jnp.int32),
        ],
        compiler_params=_params("arbitrary"),
        name="decode_scores",
    )(page_table.reshape(-1), qt, cache_kt)


def _decode_attend(logits_a, logits_b, sel, qt, kt, vt, cache_vt, page_table):
    db, n_pages, page, n_sel = _decode_dims(page_table, cache_vt)
    split = logits_a.shape[0]
    whole = lambda a: pl.BlockSpec(a.shape, lambda i, *_: (0,) * a.ndim)
    n_gather = n_sel * (MOBA_BLOCK // page)
    out_t = pl.pallas_call(
        functools.partial(_dec_attend_kernel, n_pages=n_pages, n_sel=n_sel, split=split),
        grid_spec=pltpu.PrefetchScalarGridSpec(
            num_scalar_prefetch=2,
            grid=(db,),
            in_specs=[
                pl.BlockSpec((1, n_pages, N_HEADS, page), lambda i, *_: (jnp.minimum(i, split - 1), 0, 0, 0)),
                pl.BlockSpec((1, n_pages, N_HEADS, page), lambda i, *_: (jnp.maximum(i - split, 0), 0, 0, 0)),
                whole(qt), whole(kt), whole(vt), pl.BlockSpec(memory_space=pl.ANY),
            ],
            out_specs=pl.BlockSpec((ATTN_WIDTH, db), lambda i, *_: (0, 0)),
            scratch_shapes=[
                pltpu.VMEM((2, N_HEADS, n_gather, HEAD_DIM, page), _F32),
                pltpu.VMEM((N_HEADS, db), _F32),
                pltpu.VMEM((N_HEADS, db), _F32),
                pltpu.SemaphoreType.DMA((2,)),
            ],
        ),
        out_shape=jax.ShapeDtypeStruct((ATTN_WIDTH, db), _F32),
        compiler_params=_params("arbitrary"),
        name="decode_attend",
    )(page_table.reshape(-1), sel[:, :, :, 0].reshape(-1), logits_a, logits_b, qt, kt, vt, cache_vt)
    return out_t.T


def _pool_branch(d_groups, wpg_ref, sp_ref):
    outs = []
    for g, d in enumerate(d_groups):
        cols = slice(g * POOL_GROUP_WIDTH, (g + 1) * POOL_GROUP_WIDTH)
        outs.append(_dot(d.astype(_MXU), wpg_ref[g]) * sp_ref[:, cols])
    return jnp.concatenate(outs, axis=1)


def _mix_tail(x, pool, attn, gt_ref, wbp_ref, wba_ref, wo_ref, o_ref):
    d = x.shape[1]
    m = (gt_ref[:, :d] * _dot(pool.astype(_MXU), wbp_ref[...])
         + gt_ref[:, d:] * _dot(attn.astype(_MXU), wba_ref[...]))
    o_ref[...] = x + _dot(m.astype(_MXU), wo_ref[...])


def _mix_prompt_kernel(x_ref, at_ref, u_ref, hist_ref, gt_ref, wpg_ref, sp_ref, wbp_ref, wba_ref,
                       wo_ref, o_ref, ext, *, seq):
    tm = u_ref.shape[0]
    pos0 = (pl.program_id(0) * tm) % seq
    ext[0:HIST_ROWS, :] = jnp.where(pos0 > 0, hist_ref[...], 0.0)
    ext[HIST_ROWS:, :] = u_ref[...]
    pos = pos0 + lax.broadcasted_iota(jnp.int32, (tm, 1), 0)
    d_groups = []
    for g, w in enumerate(POOL_WINDOWS):
        cols = slice(g * POOL_GROUP_WIDTH, (g + 1) * POOL_GROUP_WIDTH)
        cur = ext[HIST_ROWS:, cols]
        total = cur
        for j in range(1, w):
            total = total + ext[pl.ds(HIST_ROWS - j, tm), cols]
        cnt = jnp.minimum(w, pos + 1).astype(_F32)
        d_groups.append(total / cnt - cur)
    pool = _pool_branch(d_groups, wpg_ref, sp_ref)
    _mix_tail(x_ref[...], pool, at_ref[...], gt_ref, wbp_ref, wba_ref, wo_ref, o_ref)


def _mix_decode_kernel(x_ref, at_ref, u_ref, hist_ref, gt_ref, wpg_ref, sp_ref, wbp_ref, wba_ref,
                       wo_ref, o_ref, *, past):
    d_groups = []
    for g, w in enumerate(POOL_WINDOWS):
        cols = slice(g * POOL_GROUP_WIDTH, (g + 1) * POOL_GROUP_WIDTH)
        cur = u_ref[:, cols]
        total = cur
        for j in range(1, w):
            total = total + hist_ref[POOL_HIST - j, :, cols]
        d_groups.append(total / float(min(w, past + 1)) - cur)
    pool = _pool_branch(d_groups, wpg_ref, sp_ref)
    _mix_tail(x_ref[...], pool, at_ref[...], gt_ref, wbp_ref, wba_ref, wo_ref, o_ref)


def _mix(x, attn, u, hist, gates, wpg, sp, wbp, wba, wo, *, seq=None, past=None):
    n, d = x.shape
    tm = _token_tile(n)
    row = lambda w: pl.BlockSpec((tm, w), lambda i: (i, 0))
    if seq is not None:
        assert seq % tm == 0 and tm % HIST_ROWS == 0
        body = functools.partial(_mix_prompt_kernel, seq=seq)
        hist_spec = pl.BlockSpec(
            (HIST_ROWS, POOL_WIDTH), lambda i: (jnp.maximum(i * (tm // HIST_ROWS) - 1, 0), 0))
        scratch = [pltpu.VMEM((tm + HIST_ROWS, POOL_WIDTH), _F32)]
    else:
        body = functools.partial(_mix_decode_kernel, past=past)
        hist_spec = pl.BlockSpec((POOL_HIST, tm, POOL_WIDTH), lambda i: (0, i, 0))
        scratch = []
    return pl.pallas_call(
        body,
        grid=(n // tm,),
        in_specs=[
            row(d), row(ATTN_WIDTH), row(POOL_WIDTH), hist_spec, row(2 * d),
            _resident(wpg.shape), _resident(sp.shape), _resident(wbp.shape),
            _resident(wba.shape), _resident(wo.shape),
        ],
        out_specs=row(d),
        out_shape=jax.ShapeDtypeStruct((n, d), _F32),
        scratch_shapes=scratch,
        compiler_params=_params("parallel"),
        name="mix",
    )(x, attn, u, hist, gates, wpg, sp, wbp, wba, wo)


def kernel(x_prompt, x_sample, cache_k, cache_v, state_pool, page_table, g_ffn1, w1_ffn1, w3_ffn1,
           w2_ffn1, g_mix, w_in, b_gate, g_q, g_k, w_pool_grp, s_pool, w_branch_pool,
           w_branch_attn, w_out, g_ffn2, w1_ffn2, w3_ffn2, w2_ffn2):
    assert w_in.shape[0] == 1 and x_sample.shape[1] == 1
    b, s, d = x_prompt.shape
    db = x_sample.shape[0]
    page = cache_k.shape[2]
    past = page_table.shape[1] * page
    mx = lambda w: w[0].astype(_MXU)

    ffn1 = (g_ffn1, mx(w1_ffn1), mx(w3_ffn1), mx(w2_ffn1))
    ffn2 = (g_ffn2, mx(w1_ffn2), mx(w3_ffn2), mx(w2_ffn2))
    head = jnp.arange(2 * LANES) // HEAD_DIM
    pmat = ((head[:, None] == head[None, :]).astype(_F32) / HEAD_DIM).astype(_MXU)
    proj = (g_mix, mx(w_in), b_gate, jnp.tile(g_q, (1, N_HEADS)), jnp.tile(g_k, (1, N_HEADS)), pmat)
    post = (mx(w_pool_grp), s_pool, mx(w_branch_pool), mx(w_branch_attn), mx(w_out))

    xs = _ffn(x_sample.reshape(db, d), *ffn1)
    us, qs, _, _, gates_s, kst, vst = _proj(xs, *proj, seq=db)
    qst = qs.T
    cache_kt = jnp.transpose(cache_k[0], (0, 2, 3, 1))
    cache_vt = jnp.transpose(cache_v[0], (0, 2, 3, 1))

    steps = (b * s) // _token_tile(b * s)
    ride = db == 2 * steps
    if ride:
        _, n_pages, _, n_sel = _decode_dims(page_table, cache_kt)
        rider = lambda base: dict(pt=page_table.reshape(-1), qt=qst, cache_kt=cache_kt, base=base,
                                  n_pages=n_pages, n_sel=n_sel)
        xp, logits_a, sel_a = _ffn_scores(x_prompt.reshape(b * s, d), *ffn1, **rider(0))
    else:
        xp = _ffn(x_prompt.reshape(b * s, d), *ffn1)
    u, q, k, v, gates, kt, vt = _proj(xp, *proj, seq=s)
    shape3 = (b, s, ATTN_WIDTH)
    attn = _moba_prompt(q.reshape(shape3), k.reshape(shape3), v.reshape(shape3))
    xp = _mix(xp, attn.reshape(b * s, ATTN_WIDTH), u, u, gates, *post, seq=s)
    if ride:
        yp, logits_b, sel_b = _ffn_scores(xp, *ffn2, **rider(steps))
        sel = jnp.concatenate([sel_a, sel_b], axis=0)
    else:
        yp = _ffn(xp, *ffn2)
        logits_a, sel = _decode_scores(qst, cache_kt, page_table)
        logits_b = logits_a
    yp = yp.reshape(b, s, d)

    attn_s = _decode_attend(logits_a, logits_b, sel, qst, kst[0], vst[0], cache_vt, page_table)
    hist = jnp.transpose(state_pool[0], (1, 0, 2))
    xs = _mix(xs, attn_s, us, hist, gates_s, *post, past=past)
    ys = _ffn(xs, *ffn2).reshape(db, 1, d)

    cache_p = lambda t: jnp.transpose(t.reshape(1, b, N_HEADS, HEAD_DIM, s), (0, 1, 4, 2, 3))
    cache_s = lambda t: jnp.transpose(t.reshape(1, 1, N_HEADS, HEAD_DIM, db), (0, 4, 1, 2, 3))
    pool_p = u.reshape(b, s, POOL_WIDTH)[:, s - POOL_HIST:]
    pool_s = jnp.concatenate([state_pool[0, :, 1:], us[:, None]], axis=1)
    return (yp, ys, cache_p(kt), cache_p(vt), pool_p[None], cache_s(kst), cache_s(vst), pool_s[None])
```

```python
import functools

import jax
import jax.numpy as jnp
from jax import lax
from jax.experimental import pallas as pl
from jax.experimental.pallas import tpu as pltpu

N_HEADS = 8
HEAD_DIM = 64
ATTN_WIDTH = N_HEADS * HEAD_DIM
MOBA_BLOCK = 256
MOBA_TOP_K = 3
POOL_WINDOWS = (2, 4, 8, 16)
POOL_GROUP_WIDTH = 128
POOL_WIDTH = len(POOL_WINDOWS) * POOL_GROUP_WIDTH
POOL_HIST = max(POOL_WINDOWS) - 1
HIST_ROWS = 16
EPS = 1e-6
NEG = -1e30
SCALE = HEAD_DIM ** -0.5

LANES = 128
MXU_WIDTH = 256
TOKEN_TILE = 512
RIDE_CHUNKS = 4
VMEM_LIMIT = 56 * 1024 * 1024

_MXU = jnp.bfloat16
_F32 = jnp.float32


def _dot(a, b):
    return jnp.dot(a, b, preferred_element_type=_F32)


def _dot_nt(a, b):
    return lax.dot_general(a, b, (((1,), (1,)), ((), ())), preferred_element_type=_F32)


def _split(a):
    hi = a.astype(_MXU)
    lo = (a - hi.astype(_F32)).astype(_MXU)
    return hi, lo


def _rms(x, g):
    ms = jnp.mean(x * x, axis=-1, keepdims=True)
    return x * lax.rsqrt(ms + EPS) * g


def _params(*sem):
    return pltpu.CompilerParams(dimension_semantics=sem, vmem_limit_bytes=VMEM_LIMIT)


def _resident(shape):
    return pl.BlockSpec(shape, lambda *_: (0,) * len(shape), pipeline_mode=pl.Buffered(1))


def _token_tile(n):
    return TOKEN_TILE if n % TOKEN_TILE == 0 else n


def _column(ref, b):
    lane = lax.broadcasted_iota(jnp.int32, (1, ref.shape[1]), 1)
    return jnp.sum(jnp.where(lane == b, ref[...], 0.0), axis=1, keepdims=True)


def _top_blocks(w_ref, sel_ref, n_sel):
    _, n_pages, heads, page = w_ref.shape
    page_per_blk = MOBA_BLOCK // page
    nb = n_pages // page_per_blk
    wb = jnp.sum(w_ref[0].reshape(nb, page_per_blk, heads, page), axis=1)
    cur = jnp.sum(wb, axis=2, keepdims=True) * (1.0 / MOBA_BLOCK)
    blk_id = lax.broadcasted_iota(jnp.int32, cur.shape, 0)
    for r in range(n_sel):
        mx = jnp.max(cur, axis=0, keepdims=True)
        idx = jnp.min(jnp.where(cur == mx, blk_id, nb), axis=0, keepdims=True)
        sel_ref[0, r] = jnp.broadcast_to(idx[0], (heads, LANES))
        cur = jnp.where(blk_id == idx, -jnp.inf, cur)


def _ffn_kernel(x_ref, g_ref, w1_ref, w3_ref, w2_ref, o_ref):
    x = x_ref[...]
    h = _rms(x, g_ref[...]).astype(_MXU)
    a = _dot(h, w1_ref[...])
    b = _dot(h, w3_ref[...])
    act = (a * jax.nn.sigmoid(a) * b).astype(_MXU)
    o_ref[...] = x + 0.5 * _dot(act, w2_ref[...])


def _ffn_scores_kernel(pt_ref, x_ref, g_ref, w1_ref, w3_ref, w2_ref, qt_ref, k_hbm,
                       o_ref, w_ref, sel_ref, kbuf, qb, sem, *, base, n_pages, n_sel):
    step = pl.program_id(0)
    b = base + step
    per = n_pages // RIDE_CHUNKS

    def page_copy(bb, pg):
        return pltpu.make_async_copy(k_hbm.at[pt_ref[bb * n_pages + pg]], kbuf.at[pg], sem.at[0])

    @pl.when(step == 0)
    def _():
        for pg in range(n_pages):
            page_copy(b, pg).start()

    x = x_ref[...]
    h = _rms(x, g_ref[...]).astype(_MXU)
    a = _dot(h, w1_ref[...])
    gate = _dot(h, w3_ref[...])
    act = (a * jax.nn.sigmoid(a) * gate).astype(_MXU)

    for pg in range(n_pages):
        page_copy(b, pg).wait()
    qb[...] = jnp.broadcast_to(_column(qt_ref, b), (ATTN_WIDTH, LANES)).reshape(N_HEADS, HEAD_DIM, LANES)
    f = act.shape[1]
    tile = MXU_WIDTH if f >= RIDE_CHUNKS * MXU_WIDTH else LANES
    bounds = [min(f, -(-(f * c) // (RIDE_CHUNKS * tile)) * tile) for c in range(RIDE_CHUNKS)] + [f]
    acc = x
    for c in range(RIDE_CHUNKS):
        seen = jnp.zeros((1, LANES), _F32)
        for hd in range(N_HEADS):
            qh = qb[hd]
            for pg in range(c * per, (c + 1) * per):
                row = jnp.sum(kbuf[pg, hd] * qh, axis=0, keepdims=True)
                w_ref[0, pg, hd:hd + 1, :] = row
                seen = seen + row
        bits = pltpu.bitcast(seen, jnp.uint32)
        half = 0.5 + ((bits >> 16) >> 16).astype(_F32)
        rows = slice(bounds[c], bounds[c + 1])
        part = _dot(act[:, rows], w2_ref[rows, :])
        acc = acc + part * jnp.concatenate([half] * (x.shape[1] // LANES), axis=1)
    o_ref[...] = acc
    _top_blocks(w_ref, sel_ref, n_sel)

    @pl.when(step + 1 < pl.num_programs(0))
    def _():
        for pg in range(n_pages):
            page_copy(b + 1, pg).start()


def _ffn_scores(x, g, w1, w3, w2, pt, qt, cache_kt, *, base, n_pages, n_sel):
    n, d = x.shape
    f = w1.shape[1]
    tm = _token_tile(n)
    steps = n // tm
    page = cache_kt.shape[3]
    assert n_pages % RIDE_CHUNKS == 0 and f % LANES == 0 and f >= RIDE_CHUNKS * LANES
    fixed = lambda shape: pl.BlockSpec(shape, lambda i, *_: (0,) * len(shape), pipeline_mode=pl.Buffered(1))
    return pl.pallas_call(
        functools.partial(_ffn_scores_kernel, base=base, n_pages=n_pages, n_sel=n_sel),
        grid_spec=pltpu.PrefetchScalarGridSpec(
            num_scalar_prefetch=1,
            grid=(steps,),
            in_specs=[
                pl.BlockSpec((tm, d), lambda i, *_: (i, 0)),
                fixed((1, d)), fixed((d, f)), fixed((d, f)), fixed((f, d)), fixed(qt.shape),
                pl.BlockSpec(memory_space=pl.ANY),
            ],
            out_specs=[
                pl.BlockSpec((tm, d), lambda i, *_: (i, 0)),
                pl.BlockSpec((1, n_pages, N_HEADS, page), lambda i, *_: (i, 0, 0, 0)),
                pl.BlockSpec((1, n_sel, N_HEADS, LANES), lambda i, *_: (i, 0, 0, 0)),
            ],
            scratch_shapes=[
                pltpu.VMEM((n_pages, N_HEADS, HEAD_DIM, page), _F32),
                pltpu.VMEM((N_HEADS, HEAD_DIM, LANES), _F32),
                pltpu.SemaphoreType.DMA((1,)),
            ],
        ),
        out_shape=[
            jax.ShapeDtypeStruct((n, d), _F32),
            jax.ShapeDtypeStruct((steps, n_pages, N_HEADS, page), _F32),
            jax.ShapeDtypeStruct((steps, n_sel, N_HEADS, LANES), jnp.int32),
        ],
        compiler_params=_params("arbitrary"),
        name="ffn_scores",
    )(pt, x, g, w1, w3, w2, qt, cache_kt)


def _ffn(x, g, w1, w3, w2):
    n, d = x.shape
    f = w1.shape[1]
    tm = _token_tile(n)
    return pl.pallas_call(
        _ffn_kernel,
        grid=(n // tm,),
        in_specs=[
            pl.BlockSpec((tm, d), lambda i: (i, 0)),
            _resident((1, d)),
            _resident((d, f)),
            _resident((d, f)),
            _resident((f, d)),
        ],
        out_specs=pl.BlockSpec((tm, d), lambda i: (i, 0)),
        out_shape=jax.ShapeDtypeStruct((n, d), _F32),
        compiler_params=_params("parallel"),
        name="ffn",
    )(x, g, w1, w3, w2)


def _head_rms_t(z, g_col):
    zt = z.T
    z3 = zt.reshape(N_HEADS, HEAD_DIM, zt.shape[1])
    ms = jnp.mean(z3 * z3, axis=1, keepdims=True)
    return (z3 * lax.rsqrt(ms + EPS)).reshape(zt.shape) * g_col


def _proj_kernel(x_ref, g_ref, w_ref, b_ref, gq_ref, gk_ref,
                 u_ref, q_ref, k_ref, v_ref, gt_ref, kt_ref, vt_ref):
    h = _rms(x_ref[...], g_ref[...]).astype(_MXU)
    a = ATTN_WIDTH
    o = POOL_WIDTH
    u_ref[...] = _dot(h, w_ref[:, 0:o])
    q_ref[...] = _head_rms_t(_dot(h, w_ref[:, o:o + a]), gq_ref[...]).T
    kt = _head_rms_t(_dot(h, w_ref[:, o + a:o + 2 * a]), gk_ref[...])
    v = _dot(h, w_ref[:, o + 2 * a:o + 3 * a])
    kt_ref[0] = kt
    k_ref[...] = kt.T
    v_ref[...] = v
    vt_ref[0] = v.T
    gt_ref[...] = jax.nn.sigmoid(_dot(h, w_ref[:, o + 3 * a:]) + b_ref[...])


def _proj(x, g, w_in, b_gate, gq, gk, *, seq):
    n, d = x.shape
    wtot = w_in.shape[1]
    tm = _token_tile(n)
    assert seq % tm == 0 and n % seq == 0
    tiles_per_seq = seq // tm
    row = lambda w: pl.BlockSpec((tm, w), lambda i: (i, 0))
    tok_minor = pl.BlockSpec((1, ATTN_WIDTH, tm), lambda i: (i // tiles_per_seq, 0, i % tiles_per_seq))
    return pl.pallas_call(
        _proj_kernel,
        grid=(n // tm,),
        in_specs=[
            row(d),
            _resident((1, d)),
            _resident((d, wtot)),
            _resident((1, 2 * d)),
            _resident((ATTN_WIDTH, 1)),
            _resident((ATTN_WIDTH, 1)),
        ],
        out_specs=[row(POOL_WIDTH), row(ATTN_WIDTH), row(ATTN_WIDTH), row(ATTN_WIDTH), row(2 * d),
                   tok_minor, tok_minor],
        out_shape=[
            jax.ShapeDtypeStruct((n, POOL_WIDTH), _F32),
            jax.ShapeDtypeStruct((n, ATTN_WIDTH), _F32),
            jax.ShapeDtypeStruct((n, ATTN_WIDTH), _F32),
            jax.ShapeDtypeStruct((n, ATTN_WIDTH), _F32),
            jax.ShapeDtypeStruct((n, 2 * d), _F32),
            jax.ShapeDtypeStruct((n // seq, ATTN_WIDTH, seq), _F32),
            jax.ShapeDtypeStruct((n // seq, ATTN_WIDTH, seq), _F32),
        ],
        compiler_params=_params("parallel"),
        name="proj",
    )(x, g, w_in, b_gate, gq, gk)


Q_ROWS = 2 * MOBA_BLOCK
PREP_ROWS = 4 * MOBA_BLOCK
LOG2E = 1.4426950408889634


def _moba_kernel(q_ref, k_ref, v_ref, o_ref, kaug, vaug, qaug, km, s_buf, mrun, mb, acc_s, *, nb, n_sel):
    step = pl.program_id(2)
    blk = MOBA_BLOCK
    blk_shift = blk.bit_length() - 1
    half = HEAD_DIM
    lane = lax.broadcasted_iota(jnp.int32, (1, LANES), 1)
    in_head = [(lane >= half * h) & (lane < half * (h + 1)) for h in range(2)]

    @pl.when(step == 0)
    def _():
        km[...] = jnp.mean(k_ref[0].reshape(nb, blk, LANES), axis=1)
        km_split = [_split(jnp.where(in_head[h], km[...], 0.0)) for h in range(2)]

        def prep(c, carry):
            r0 = pl.multiple_of(c * PREP_ROWS, PREP_ROWS)
            rows = pl.ds(r0, PREP_ROWS)
            for j in range(PREP_ROWS // blk):
                rows_j = pl.ds(r0 + j * blk, blk)
                for h in range(2):
                    onehot = (lane - half * (1 - h) == c * (PREP_ROWS // blk) + j).astype(_F32)
                    kaug[h, rows_j, :] = jnp.where(in_head[h], k_ref[0, rows_j, :], onehot).astype(_MXU)
            v = v_ref[0, rows, :]
            q2 = q_ref[0, rows, :]
            q_hi, q_lo = _split(q2)
            blk_id = lax.broadcasted_iota(jnp.int32, (nb, PREP_ROWS), 0)
            own = lax.shift_right_logical(
                r0 + lax.broadcasted_iota(jnp.int32, (nb, PREP_ROWS), 1), blk_shift)
            past = blk_id < own
            for h in range(2):
                vaug[h, rows, :] = jnp.where(in_head[h], v, 1.0).astype(_MXU)
                km_hi, km_lo = km_split[h]
                sb = _dot_nt(km_hi, q_hi) + (_dot_nt(km_lo, q_hi) + _dot_nt(km_hi, q_lo))
                cur = jnp.where(past, sb, NEG)
                sel = jnp.zeros(cur.shape, jnp.bool_)
                for _ in range(n_sel):
                    mx = jnp.max(cur, axis=0, keepdims=True)
                    idx = jnp.min(jnp.where(cur == mx, blk_id, nb), axis=0, keepdims=True)
                    pick = blk_id == idx
                    sel = sel | pick
                    cur = jnp.where(pick, -jnp.inf, cur)
                bias_t = jnp.where((sel & past) | (blk_id == own), 0.0, NEG)
                before = half * (1 - h)
                parts = [bias_t, jnp.zeros((LANES - before - nb, PREP_ROWS), _F32)]
                if before:
                    parts.insert(0, jnp.zeros((before, PREP_ROWS), _F32))
                bias_lanes = jnp.concatenate(parts, axis=0).T
                qaug[h, rows, :] = jnp.where(in_head[h], q2 * (SCALE * LOG2E), bias_lanes).astype(_MXU)
            return carry

        lax.fori_loop(0, k_ref.shape[1] // PREP_ROWS, prep, 0)

    def rows_of(ref, h, t, n=1):
        return ref[h, pl.ds(pl.multiple_of(t * Q_ROWS, Q_ROWS), n * Q_ROWS), :]

    def lane_max(s):
        m = s[:, :LANES]
        for c in range(1, s.shape[1] // LANES):
            m = jnp.maximum(m, s[:, c * LANES:(c + 1) * LANES])
        return m

    groups = step + 1
    odd = groups % 2 == 1

    key_ahead = (lax.broadcasted_iota(jnp.int32, (Q_ROWS, 2 * Q_ROWS), 1)
                 - lax.broadcasted_iota(jnp.int32, (Q_ROWS, 2 * Q_ROWS), 0))
    for h in range(2):
        mrun[h] = jnp.full((Q_ROWS, LANES), NEG, _F32)

    def scores(t, n, causal):
        for h in range(2):
            s = _dot_nt(rows_of(qaug, h, step), rows_of(kaug, h, t, n))
            if causal:
                s = jnp.where(key_ahead[:, :n * Q_ROWS] <= (step - t) * Q_ROWS, s, NEG)
            for c in range(n):
                s_buf[h, t + c] = s[:, c * Q_ROWS:(c + 1) * Q_ROWS]
            mrun[h] = jnp.maximum(mrun[h], lane_max(s))

    def scores_pair(tt, carry):
        scores(2 * tt, 2, False)
        return carry

    lax.fori_loop(0, (groups - 1) // 2, scores_pair, 0)

    @pl.when(odd)
    def _():
        scores(step, 1, True)

    @pl.when(jnp.logical_not(odd))
    def _():
        scores(step - 1, 2, True)

    for h in range(2):
        mb[h] = jnp.broadcast_to(jnp.max(mrun[h], axis=1, keepdims=True), (Q_ROWS, LANES))
        acc_s[h] = jnp.zeros((Q_ROWS, LANES), _F32)

    def attend(t, n):
        for h in range(2):
            s = jnp.concatenate([s_buf[h, t + c] for c in range(n)], axis=1)
            m = jnp.concatenate([mb[h]] * (s.shape[1] // LANES), axis=1)
            acc_s[h] = acc_s[h] + _dot(jnp.exp2(s - m).astype(_MXU), rows_of(vaug, h, t, n))

    def attend_pair(tt, carry):
        attend(2 * tt, 2)
        return carry

    lax.fori_loop(0, groups // 2, attend_pair, 0)

    @pl.when(odd)
    def _():
        attend(step, 1)

    num = jnp.where(in_head[0], acc_s[0], acc_s[1])
    den = pltpu.roll(jnp.where(in_head[0], acc_s[1], acc_s[0]), half, axis=1)
    o_ref[0] = num / den


def _moba_prompt(q, k, v):
    b, s, w = q.shape
    assert w == ATTN_WIDTH and s % PREP_ROWS == 0 and MOBA_BLOCK & (MOBA_BLOCK - 1) == 0
    nb = s // MOBA_BLOCK
    assert nb <= HEAD_DIM and nb % 8 == 0
    n_sel = min(MOBA_TOP_K, nb - 1)
    whole = pl.BlockSpec((1, s, LANES), lambda bi, p, i: (bi, 0, p))
    return pl.pallas_call(
        functools.partial(_moba_kernel, nb=nb, n_sel=n_sel),
        grid=(b, w // LANES, s // Q_ROWS),
        in_specs=[whole, whole, whole],
        out_specs=pl.BlockSpec((1, Q_ROWS, LANES), lambda bi, p, i: (bi, i, p)),
        out_shape=jax.ShapeDtypeStruct((b, s, w), _F32),
        scratch_shapes=[
            pltpu.VMEM((2, s, LANES), _MXU),
            pltpu.VMEM((2, s, LANES), _MXU),
            pltpu.VMEM((2, s, LANES), _MXU),
            pltpu.VMEM((nb, LANES), _F32),
            pltpu.VMEM((2, s // Q_ROWS, Q_ROWS, Q_ROWS), _F32),
            pltpu.VMEM((2, Q_ROWS, LANES), _F32),
            pltpu.VMEM((2, Q_ROWS, LANES), _F32),
            pltpu.VMEM((2, Q_ROWS, LANES), _F32),
        ],
        compiler_params=_params("parallel", "parallel", "arbitrary"),
        name="moba_prompt",
    )(q, k, v)


DEC_CHUNK = 8
DEC_SLOTS = 4
GATHER_SLOTS = 3


def _dec_scores_kernel(pt_ref, qt_ref, k_hbm, w_ref, sel_ref, kbuf, qb, sem, *, n_pages, n_sel):
    b = pl.program_id(0)
    n_chunks = n_pages // DEC_CHUNK
    total = pl.num_programs(0) * n_chunks

    def page_copy(g, pg):
        phys = pt_ref[g * DEC_CHUNK + pg]
        slot = g % DEC_SLOTS
        return pltpu.make_async_copy(k_hbm.at[phys], kbuf.at[slot, pg], sem.at[slot])

    def start_chunk(g):
        for pg in range(DEC_CHUNK):
            page_copy(g, pg).start()

    @pl.when(b == 0)
    def _():
        for g in range(DEC_SLOTS - 1):
            start_chunk(g)

    qb[...] = jnp.broadcast_to(_column(qt_ref, b), (ATTN_WIDTH, LANES)).reshape(N_HEADS, HEAD_DIM, LANES)

    def chunk_body(c, carry):
        g = b * n_chunks + c
        nxt = g + DEC_SLOTS - 1

        @pl.when(nxt < total)
        def _():
            start_chunk(nxt)

        for pg in range(DEC_CHUNK):
            page_copy(g, pg).wait()
        slot = g % DEC_SLOTS
        for pg in range(DEC_CHUNK):
            w = jnp.sum(kbuf[slot, pg] * qb[...], axis=1)
            w_ref[0, pl.ds(c * DEC_CHUNK + pg, 1)] = w[None]
        return carry

    lax.fori_loop(0, n_chunks, chunk_body, 0)
    _top_blocks(w_ref, sel_ref, n_sel)


def _dec_attend_kernel(pt_ref, sel_ref, wa_ref, wb_ref, qt_ref, kt_ref, vt_ref, v_hbm, o_ref,
                       vbuf, own_s, coef_s, sem, *, n_pages, n_sel, split):
    b = pl.program_id(0)
    last = pl.num_programs(0) - 1
    page_per_blk = MOBA_BLOCK // v_hbm.shape[3]
    n_gather = n_sel * page_per_blk

    def chosen_page(bb, h, g):
        blk = sel_ref[(bb * n_sel + g // page_per_blk) * N_HEADS + h]
        return blk * page_per_blk + g % page_per_blk

    def copies(bb, slot):
        return [pltpu.make_async_copy(v_hbm.at[pt_ref[bb * n_pages + chosen_page(bb, h, g)], h],
                                      vbuf.at[slot, h, g], sem.at[slot])
                for h in range(N_HEADS) for g in range(n_gather)]

    @pl.when(b == 0)
    def _():
        o_ref[...] = jnp.zeros(o_ref.shape, _F32)
        coef_s[...] = jnp.zeros(coef_s.shape, _F32)
        qk = qt_ref[...] * kt_ref[...]
        own_s[...] = jnp.sum(qk.reshape(N_HEADS, HEAD_DIM, qk.shape[1]), axis=1) * SCALE
        for ahead in range(GATHER_SLOTS - 1):
            for c in copies(ahead, ahead):
                c.start()

    @pl.when(b + GATHER_SLOTS - 1 <= last)
    def _():
        for c in copies(b + GATHER_SLOTS - 1, (b + GATHER_SLOTS - 1) % GATHER_SLOTS):
            c.start()

    slot = b % GATHER_SLOTS
    for c in copies(b, slot):
        c.wait()

    lane = lax.broadcasted_iota(jnp.int32, (1, o_ref.shape[1]), 1)
    head = lax.broadcasted_iota(jnp.int32, (N_HEADS, 1), 0)
    here = lane == b
    s_own = jnp.sum(jnp.where(here, own_s[...], 0.0), axis=1, keepdims=True)
    tiles = []
    for g in range(n_gather):
        t = jnp.zeros((N_HEADS, wa_ref.shape[3]), _F32)
        for h in range(N_HEADS):
            pg = pl.ds(chosen_page(b, h, g), 1)
            t = jnp.where(head == h, jnp.where(b < split, wa_ref[0, pg][0], wb_ref[0, pg][0]), t)
        tiles.append(t * SCALE)
    m = s_own
    for t in tiles:
        m = jnp.maximum(m, jnp.max(t, axis=1, keepdims=True))
    p_own = jnp.exp(s_own - m)
    denom = p_own
    probs = []
    for t in tiles:
        p = jnp.exp(t - m)
        denom = denom + jnp.sum(p, axis=1, keepdims=True)
        probs.append(p)
    inv = 1.0 / denom
    for h in range(N_HEADS):
        acc = jnp.zeros((HEAD_DIM, probs[0].shape[1]), _F32)
        for g in range(n_gather):
            acc = acc + probs[g][h:h + 1, :] * vbuf[slot, h, g]
        hs = slice(h * HEAD_DIM, (h + 1) * HEAD_DIM)
        o_ref[hs, :] = jnp.where(here, jnp.sum(acc, axis=1, keepdims=True) * inv[h:h + 1, :], o_ref[hs, :])
    coef_s[...] = jnp.where(here, p_own * inv, coef_s[...])

    @pl.when(b == last)
    def _():
        for h in range(N_HEADS):
            hs = slice(h * HEAD_DIM, (h + 1) * HEAD_DIM)
            o_ref[hs, :] = o_ref[hs, :] + coef_s[h:h + 1, :] * vt_ref[hs, :]


def _decode_dims(page_table, cache_kt):
    db, n_pages = page_table.shape
    page = cache_kt.shape[3]
    assert MOBA_BLOCK % page == 0 and (n_pages * page) % MOBA_BLOCK == 0
    n_sel = min(MOBA_TOP_K, n_pages * page // MOBA_BLOCK)
    assert n_sel > 0
    return db, n_pages, page, n_sel


def _decode_scores(qt, cache_kt, page_table):
    db, n_pages, page, n_sel = _decode_dims(page_table, cache_kt)
    assert n_pages % DEC_CHUNK == 0 and db * (n_pages // DEC_CHUNK) >= DEC_SLOTS
    return pl.pallas_call(
        functools.partial(_dec_scores_kernel, n_pages=n_pages, n_sel=n_sel),
        grid_spec=pltpu.PrefetchScalarGridSpec(
            num_scalar_prefetch=1,
            grid=(db,),
            in_specs=[pl.BlockSpec(qt.shape, lambda i, *_: (0, 0)), pl.BlockSpec(memory_space=pl.ANY)],
            out_specs=[
                pl.BlockSpec((1, n_pages, N_HEADS, page), lambda i, *_: (i, 0, 0, 0)),
                pl.BlockSpec((1, n_sel, N_HEADS, LANES), lambda i, *_: (i, 0, 0, 0)),
            ],
            scratch_shapes=[
                pltpu.VMEM((DEC_SLOTS, DEC_CHUNK, N_HEADS, HEAD_DIM, page), _F32),
                pltpu.VMEM((N_HEADS, HEAD_DIM, LANES), _F32),
                pltpu.SemaphoreType.DMA((DEC_SLOTS,)),
            ],
        ),
        out_shape=[
            jax.ShapeDtypeStruct((db, n_pages, N_HEADS, page), _F32),
            jax.ShapeDtypeStruct((db, n_sel, N_HEADS, LANES), jnp.int32),
        ],
        compiler_params=_params("arbitrary"),
        name="decode_scores",
    )(page_table.reshape(-1), qt, cache_kt)


def _decode_attend(logits_a, logits_b, sel, qt, kt, vt, cache_vt, page_table):
    db, n_pages, page, n_sel = _decode_dims(page_table, cache_vt)
    split = logits_a.shape[0]
    assert db >= GATHER_SLOTS
    whole = lambda a: pl.BlockSpec(a.shape, lambda i, *_: (0,) * a.ndim)
    n_gather = n_sel * (MOBA_BLOCK // page)
    out_t = pl.pallas_call(
        functools.partial(_dec_attend_kernel, n_pages=n_pages, n_sel=n_sel, split=split),
        grid_spec=pltpu.PrefetchScalarGridSpec(
            num_scalar_prefetch=2,
            grid=(db,),
            in_specs=[
                pl.BlockSpec((1, n_pages, N_HEADS, page), lambda i, *_: (jnp.minimum(i, split - 1), 0, 0, 0)),
                pl.BlockSpec((1, n_pages, N_HEADS, page), lambda i, *_: (jnp.maximum(i - split, 0), 0, 0, 0)),
                whole(qt), whole(kt), whole(vt), pl.BlockSpec(memory_space=pl.ANY),
            ],
            out_specs=pl.BlockSpec((ATTN_WIDTH, db), lambda i, *_: (0, 0)),
            scratch_shapes=[
                pltpu.VMEM((GATHER_SLOTS, N_HEADS, n_gather, HEAD_DIM, page), _F32),
                pltpu.VMEM((N_HEADS, db), _F32),
                pltpu.VMEM((N_HEADS, db), _F32),
                pltpu.SemaphoreType.DMA((GATHER_SLOTS,)),
            ],
        ),
        out_shape=jax.ShapeDtypeStruct((ATTN_WIDTH, db), _F32),
        compiler_params=_params("arbitrary"),
        name="decode_attend",
    )(page_table.reshape(-1), sel[:, :, :, 0].reshape(-1), logits_a, logits_b, qt, kt, vt, cache_vt)
    return out_t.T


def _pool_branch(d_groups, wpg_ref, sp_ref):
    outs = []
    for g, d in enumerate(d_groups):
        cols = slice(g * POOL_GROUP_WIDTH, (g + 1) * POOL_GROUP_WIDTH)
        outs.append(_dot(d.astype(_MXU), wpg_ref[g]) * sp_ref[:, cols])
    return jnp.concatenate(outs, axis=1)


def _mix_tail(x, pool, attn, gt_ref, wbp_ref, wba_ref, wo_ref, o_ref):
    d = x.shape[1]
    m = (gt_ref[:, :d] * _dot(pool.astype(_MXU), wbp_ref[...])
         + gt_ref[:, d:] * _dot(attn.astype(_MXU), wba_ref[...]))
    o_ref[...] = x + _dot(m.astype(_MXU), wo_ref[...])


def _mix_prompt_kernel(x_ref, at_ref, u_ref, hist_ref, gt_ref, wpg_ref, sp_ref, wbp_ref, wba_ref,
                       wo_ref, o_ref, ext, lvl, *, seq):
    tm = u_ref.shape[0]
    pad = 8
    n = tm + HIST_ROWS
    pos0 = (pl.program_id(0) * tm) % seq
    ext[0:pad, :] = jnp.zeros((pad, POOL_WIDTH), _F32)
    ext[pad:pad + HIST_ROWS, :] = jnp.where(pos0 > 0, hist_ref[...], 0.0)
    ext[pad + HIST_ROWS:, :] = u_ref[...]
    lvl[:, 0:pad, :] = jnp.zeros((2, pad, POOL_GROUP_WIDTH), _F32)
    pos = pos0 + lax.broadcasted_iota(jnp.int32, (tm, 1), 0)
    d_groups = []
    for g, w in enumerate(POOL_WINDOWS):
        assert w & (w - 1) == 0 and w <= HIST_ROWS
        cols = slice(g * POOL_GROUP_WIDTH, (g + 1) * POOL_GROUP_WIDTH)
        read = lambda start: ext[pl.ds(start, n), cols]
        shift = 1
        while True:
            total = read(pad) + read(pad - shift)
            shift *= 2
            if shift == w:
                break
            slot = lvl.at[(shift.bit_length()) % 2]
            slot[pl.ds(pad, n), :] = total
            read = lambda start, slot=slot: slot[pl.ds(start, n), :]
        cur = ext[pad + HIST_ROWS:, cols]
        cnt = jnp.minimum(w, pos + 1).astype(_F32)
        d_groups.append(total[HIST_ROWS:] / cnt - cur)
    pool = _pool_branch(d_groups, wpg_ref, sp_ref)
    _mix_tail(x_ref[...], pool, at_ref[...], gt_ref, wbp_ref, wba_ref, wo_ref, o_ref)


def _mix_decode_kernel(x_ref, at_ref, u_ref, hist_ref, gt_ref, wpg_ref, sp_ref, wbp_ref, wba_ref,
                       wo_ref, o_ref, *, past):
    d_groups = []
    for g, w in enumerate(POOL_WINDOWS):
        cols = slice(g * POOL_GROUP_WIDTH, (g + 1) * POOL_GROUP_WIDTH)
        cur = u_ref[:, cols]
        total = cur
        for j in range(1, w):
            total = total + hist_ref[POOL_HIST - j, :, cols]
        d_groups.append(total / float(min(w, past + 1)) - cur)
    pool = _pool_branch(d_groups, wpg_ref, sp_ref)
    _mix_tail(x_ref[...], pool, at_ref[...], gt_ref, wbp_ref, wba_ref, wo_ref, o_ref)


def _mix(x, attn, u, hist, gates, wpg, sp, wbp, wba, wo, *, seq=None, past=None):
    n, d = x.shape
    tm = _token_tile(n)
    row = lambda w: pl.BlockSpec((tm, w), lambda i: (i, 0))
    if seq is not None:
        assert seq % tm == 0 and tm % HIST_ROWS == 0
        body = functools.partial(_mix_prompt_kernel, seq=seq)
        hist_spec = pl.BlockSpec(
            (HIST_ROWS, POOL_WIDTH), lambda i: (jnp.maximum(i * (tm // HIST_ROWS) - 1, 0), 0))
        scratch = [pltpu.VMEM((8 + HIST_ROWS + tm, POOL_WIDTH), _F32),
                   pltpu.VMEM((2, 8 + HIST_ROWS + tm, POOL_GROUP_WIDTH), _F32)]
    else:
        body = functools.partial(_mix_decode_kernel, past=past)
        hist_spec = pl.BlockSpec((POOL_HIST, tm, POOL_WIDTH), lambda i: (0, i, 0))
        scratch = []
    return pl.pallas_call(
        body,
        grid=(n // tm,),
        in_specs=[
            row(d), row(ATTN_WIDTH), row(POOL_WIDTH), hist_spec, row(2 * d),
            _resident(wpg.shape), _resident(sp.shape), _resident(wbp.shape),
            _resident(wba.shape), _resident(wo.shape),
        ],
        out_specs=row(d),
        out_shape=jax.ShapeDtypeStruct((n, d), _F32),
        scratch_shapes=scratch,
        compiler_params=_params("parallel"),
        name="mix",
    )(x, attn, u, hist, gates, wpg, sp, wbp, wba, wo)


def kernel(x_prompt, x_sample, cache_k, cache_v, state_pool, page_table, g_ffn1, w1_ffn1, w3_ffn1,
           w2_ffn1, g_mix, w_in, b_gate, g_q, g_k, w_pool_grp, s_pool, w_branch_pool,
           w_branch_attn, w_out, g_ffn2, w1_ffn2, w3_ffn2, w2_ffn2):
    assert w_in.shape[0] == 1 and x_sample.shape[1] == 1
    b, s, d = x_prompt.shape
    db = x_sample.shape[0]
    page = cache_k.shape[2]
    past = page_table.shape[1] * page
    mx = lambda w: w[0].astype(_MXU)

    ffn1 = (g_ffn1, mx(w1_ffn1), mx(w3_ffn1), mx(w2_ffn1))
    ffn2 = (g_ffn2, mx(w1_ffn2), mx(w3_ffn2), mx(w2_ffn2))
    head_gain = lambda g: jnp.tile(g, (1, N_HEADS)).reshape(ATTN_WIDTH, 1)
    proj = (g_mix, mx(w_in), b_gate, head_gain(g_q), head_gain(g_k))
    post = (mx(w_pool_grp), s_pool, mx(w_branch_pool), mx(w_branch_attn), mx(w_out))

    xs = _ffn(x_sample.reshape(db, d), *ffn1)
    us, qs, _, _, gates_s, kst, vst = _proj(xs, *proj, seq=db)
    qst = qs.T
    cache_kt = jnp.transpose(cache_k[0], (0, 2, 3, 1))
    cache_vt = jnp.transpose(cache_v[0], (0, 2, 3, 1))

    steps = (b * s) // _token_tile(b * s)
    ride = db == 2 * steps
    if ride:
        _, n_pages, _, n_sel = _decode_dims(page_table, cache_kt)
        rider = lambda base: dict(pt=page_table.reshape(-1), qt=qst, cache_kt=cache_kt, base=base,
                                  n_pages=n_pages, n_sel=n_sel)
        xp, logits_a, sel_a = _ffn_scores(x_prompt.reshape(b * s, d), *ffn1, **rider(0))
    else:
        xp = _ffn(x_prompt.reshape(b * s, d), *ffn1)
    u, q, k, v, gates, kt, vt = _proj(xp, *proj, seq=s)
    shape3 = (b, s, ATTN_WIDTH)
    attn = _moba_prompt(q.reshape(shape3), k.reshape(shape3), v.reshape(shape3))
    xp = _mix(xp, attn.reshape(b * s, ATTN_WIDTH), u, u, gates, *post, seq=s)
    if ride:
        yp, logits_b, sel_b = _ffn_scores(xp, *ffn2, **rider(steps))
        sel = jnp.concatenate([sel_a, sel_b], axis=0)
    else:
        yp = _ffn(xp, *ffn2)
        logits_a, sel = _decode_scores(qst, cache_kt, page_table)
        logits_b = logits_a
    yp = yp.reshape(b, s, d)

    attn_s = _decode_attend(logits_a, logits_b, sel, qst, kst[0], vst[0], cache_vt, page_table)
    hist = jnp.transpose(state_pool[0], (1, 0, 2))
    xs = _mix(xs, attn_s, us, hist, gates_s, *post, past=past)
    ys = _ffn(xs, *ffn2).reshape(db, 1, d)

    cache_p = lambda t: jnp.transpose(t.reshape(1, b, N_HEADS, HEAD_DIM, s), (0, 1, 4, 2, 3))
    cache_s = lambda t: jnp.transpose(t.reshape(1, 1, N_HEADS, HEAD_DIM, db), (0, 4, 1, 2, 3))
    pool_p = u.reshape(b, s, POOL_WIDTH)[:, s - POOL_HIST:]
    pool_s = jnp.concatenate([state_pool[0, :, 1:], us[:, None]], axis=1)
    return (yp, ys, cache_p(kt), cache_p(vt), pool_p[None], cache_s(kst), cache_s(vst), pool_s[None])
```

```python
import functools

import jax
import jax.numpy as jnp
from jax import lax
from jax.experimental import pallas as pl
from jax.experimental.pallas import tpu as pltpu

N_HEADS = 8
HEAD_DIM = 64
ATTN_WIDTH = N_HEADS * HEAD_DIM
MOBA_BLOCK = 256
MOBA_TOP_K = 3
POOL_WINDOWS = (2, 4, 8, 16)
POOL_GROUP_WIDTH = 128
POOL_WIDTH = len(POOL_WINDOWS) * POOL_GROUP_WIDTH
POOL_HIST = max(POOL_WINDOWS) - 1
HIST_ROWS = 16
EPS = 1e-6
NEG = -1e30
SCALE = HEAD_DIM ** -0.5

LANES = 128
MXU_WIDTH = 256
TOKEN_TILE = 512
RIDE_CHUNKS = 4
VMEM_LIMIT = 56 * 1024 * 1024

_MXU = jnp.bfloat16
_F32 = jnp.float32


def _dot(a, b):
    return jnp.dot(a, b, preferred_element_type=_F32)


def _dot_nt(a, b):
    return lax.dot_general(a, b, (((1,), (1,)), ((), ())), preferred_element_type=_F32)


def _split(a):
    hi = a.astype(_MXU)
    lo = (a - hi.astype(_F32)).astype(_MXU)
    return hi, lo


def _rms(x, g):
    ms = jnp.mean(x * x, axis=-1, keepdims=True)
    return x * lax.rsqrt(ms + EPS) * g


def _params(*sem):
    return pltpu.CompilerParams(dimension_semantics=sem, vmem_limit_bytes=VMEM_LIMIT)


def _resident(shape):
    return pl.BlockSpec(shape, lambda *_: (0,) * len(shape), pipeline_mode=pl.Buffered(1))


def _token_tile(n):
    return TOKEN_TILE if n % TOKEN_TILE == 0 else n


def _column(ref, b):
    lane = lax.broadcasted_iota(jnp.int32, (1, ref.shape[1]), 1)
    return jnp.sum(jnp.where(lane == b, ref[...], 0.0), axis=1, keepdims=True)


def _top_blocks(w_ref, sel_ref, n_sel):
    _, n_pages, heads, page = w_ref.shape
    page_per_blk = MOBA_BLOCK // page
    nb = n_pages // page_per_blk
    wb = jnp.sum(w_ref[0].reshape(nb, page_per_blk, heads, page), axis=1)
    cur = jnp.sum(wb, axis=2, keepdims=True) * (1.0 / MOBA_BLOCK)
    blk_id = lax.broadcasted_iota(jnp.int32, cur.shape, 0)
    for r in range(n_sel):
        mx = jnp.max(cur, axis=0, keepdims=True)
        idx = jnp.min(jnp.where(cur == mx, blk_id, nb), axis=0, keepdims=True)
        sel_ref[0, r] = jnp.broadcast_to(idx[0], (heads, LANES))
        cur = jnp.where(blk_id == idx, -jnp.inf, cur)


def _ffn_kernel(x_ref, g_ref, w1_ref, w3_ref, w2_ref, o_ref):
    x = x_ref[...]
    h = _rms(x, g_ref[...]).astype(_MXU)
    a = _dot(h, w1_ref[...])
    b = _dot(h, w3_ref[...])
    act = (a * jax.nn.sigmoid(a) * b).astype(_MXU)
    o_ref[...] = x + 0.5 * _dot(act, w2_ref[...])


def _ffn_scores_kernel(pt_ref, x_ref, g_ref, w1_ref, w3_ref, w2_ref, qt_ref, k_hbm,
                       o_ref, w_ref, sel_ref, kbuf, qb, sem, *, base, n_pages, n_sel):
    step = pl.program_id(0)
    b = base + step
    per = n_pages // RIDE_CHUNKS

    def page_copy(bb, pg):
        return pltpu.make_async_copy(k_hbm.at[pt_ref[bb * n_pages + pg]], kbuf.at[pg], sem.at[0])

    @pl.when(step == 0)
    def _():
        for pg in range(n_pages):
            page_copy(b, pg).start()

    x = x_ref[...]
    h = _rms(x, g_ref[...]).astype(_MXU)
    a = _dot(h, w1_ref[...])
    gate = _dot(h, w3_ref[...])
    act = (a * jax.nn.sigmoid(a) * gate).astype(_MXU)

    for pg in range(n_pages):
        page_copy(b, pg).wait()
    qb[...] = jnp.broadcast_to(_column(qt_ref, b), (ATTN_WIDTH, LANES)).reshape(N_HEADS, HEAD_DIM, LANES)
    f = act.shape[1]
    tile = MXU_WIDTH if f >= RIDE_CHUNKS * MXU_WIDTH else LANES
    bounds = [min(f, -(-(f * c) // (RIDE_CHUNKS * tile)) * tile) for c in range(RIDE_CHUNKS)] + [f]
    acc = x
    for c in range(RIDE_CHUNKS):
        seen = jnp.zeros((1, LANES), _F32)
        for hd in range(N_HEADS):
            qh = qb[hd]
            for pg in range(c * per, (c + 1) * per):
                row = jnp.sum(kbuf[pg, hd] * qh, axis=0, keepdims=True)
                w_ref[0, pg, hd:hd + 1, :] = row
                seen = seen + row
        bits = pltpu.bitcast(seen, jnp.uint32)
        half = 0.5 + ((bits >> 16) >> 16).astype(_F32)
        rows = slice(bounds[c], bounds[c + 1])
        part = _dot(act[:, rows], w2_ref[rows, :])
        acc = acc + part * jnp.concatenate([half] * (x.shape[1] // LANES), axis=1)
    o_ref[...] = acc
    _top_blocks(w_ref, sel_ref, n_sel)

    @pl.when(step + 1 < pl.num_programs(0))
    def _():
        for pg in range(n_pages):
            page_copy(b + 1, pg).start()


def _ffn_scores(x, g, w1, w3, w2, pt, qt, cache_kt, *, base, n_pages, n_sel):
    n, d = x.shape
    f = w1.shape[1]
    tm = _token_tile(n)
    steps = n // tm
    page = cache_kt.shape[3]
    assert n_pages % RIDE_CHUNKS == 0 and f % LANES == 0 and f >= RIDE_CHUNKS * LANES
    fixed = lambda shape: pl.BlockSpec(shape, lambda i, *_: (0,) * len(shape), pipeline_mode=pl.Buffered(1))
    return pl.pallas_call(
        functools.partial(_ffn_scores_kernel, base=base, n_pages=n_pages, n_sel=n_sel),
        grid_spec=pltpu.PrefetchScalarGridSpec(
            num_scalar_prefetch=1,
            grid=(steps,),
            in_specs=[
                pl.BlockSpec((tm, d), lambda i, *_: (i, 0)),
                fixed((1, d)), fixed((d, f)), fixed((d, f)), fixed((f, d)), fixed(qt.shape),
                pl.BlockSpec(memory_space=pl.ANY),
            ],
            out_specs=[
                pl.BlockSpec((tm, d), lambda i, *_: (i, 0)),
                pl.BlockSpec((1, n_pages, N_HEADS, page), lambda i, *_: (i, 0, 0, 0)),
                pl.BlockSpec((1, n_sel, N_HEADS, LANES), lambda i, *_: (i, 0, 0, 0)),
            ],
            scratch_shapes=[
                pltpu.VMEM((n_pages, N_HEADS, HEAD_DIM, page), _F32),
                pltpu.VMEM((N_HEADS, HEAD_DIM, LANES), _F32),
                pltpu.SemaphoreType.DMA((1,)),
            ],
        ),
        out_shape=[
            jax.ShapeDtypeStruct((n, d), _F32),
            jax.ShapeDtypeStruct((steps, n_pages, N_HEADS, page), _F32),
            jax.ShapeDtypeStruct((steps, n_sel, N_HEADS, LANES), jnp.int32),
        ],
        compiler_params=_params("arbitrary"),
        name="ffn_scores",
    )(pt, x, g, w1, w3, w2, qt, cache_kt)


def _ffn(x, g, w1, w3, w2):
    n, d = x.shape
    f = w1.shape[1]
    tm = _token_tile(n)
    return pl.pallas_call(
        _ffn_kernel,
        grid=(n // tm,),
        in_specs=[
            pl.BlockSpec((tm, d), lambda i: (i, 0)),
            _resident((1, d)),
            _resident((d, f)),
            _resident((d, f)),
            _resident((f, d)),
        ],
        out_specs=pl.BlockSpec((tm, d), lambda i: (i, 0)),
        out_shape=jax.ShapeDtypeStruct((n, d), _F32),
        compiler_params=_params("parallel"),
        name="ffn",
    )(x, g, w1, w3, w2)


def _head_rms_t(z, g_col):
    zt = z.T
    z3 = zt.reshape(N_HEADS, HEAD_DIM, zt.shape[1])
    ms = jnp.mean(z3 * z3, axis=1, keepdims=True)
    return (z3 * lax.rsqrt(ms + EPS)).reshape(zt.shape) * g_col


def _proj_kernel(x_ref, g_ref, w_ref, b_ref, gq_ref, gk_ref,
                 u_ref, q_ref, k_ref, v_ref, gt_ref, kt_ref, vt_ref):
    h = _rms(x_ref[...], g_ref[...]).astype(_MXU)
    a = ATTN_WIDTH
    o = POOL_WIDTH
    u_ref[...] = _dot(h, w_ref[:, 0:o])
    q_ref[...] = _head_rms_t(_dot(h, w_ref[:, o:o + a]), gq_ref[...]).T
    kt = _head_rms_t(_dot(h, w_ref[:, o + a:o + 2 * a]), gk_ref[...])
    v = _dot(h, w_ref[:, o + 2 * a:o + 3 * a])
    kt_ref[0] = kt
    k_ref[...] = kt.T
    v_ref[...] = v
    vt_ref[0] = v.T
    gt_ref[...] = jax.nn.sigmoid(_dot(h, w_ref[:, o + 3 * a:]) + b_ref[...])


def _proj(x, g, w_in, b_gate, gq, gk, *, seq):
    n, d = x.shape
    wtot = w_in.shape[1]
    tm = _token_tile(n)
    assert seq % tm == 0 and n % seq == 0
    tiles_per_seq = seq // tm
    row = lambda w: pl.BlockSpec((tm, w), lambda i: (i, 0))
    tok_minor = pl.BlockSpec((1, ATTN_WIDTH, tm), lambda i: (i // tiles_per_seq, 0, i % tiles_per_seq))
    return pl.pallas_call(
        _proj_kernel,
        grid=(n // tm,),
        in_specs=[
            row(d),
            _resident((1, d)),
            _resident((d, wtot)),
            _resident((1, 2 * d)),
            _resident((ATTN_WIDTH, 1)),
            _resident((ATTN_WIDTH, 1)),
        ],
        out_specs=[row(POOL_WIDTH), row(ATTN_WIDTH), row(ATTN_WIDTH), row(ATTN_WIDTH), row(2 * d),
                   tok_minor, tok_minor],
        out_shape=[
            jax.ShapeDtypeStruct((n, POOL_WIDTH), _F32),
            jax.ShapeDtypeStruct((n, ATTN_WIDTH), _F32),
            jax.ShapeDtypeStruct((n, ATTN_WIDTH), _F32),
            jax.ShapeDtypeStruct((n, ATTN_WIDTH), _F32),
            jax.ShapeDtypeStruct((n, 2 * d), _F32),
            jax.ShapeDtypeStruct((n // seq, ATTN_WIDTH, seq), _F32),
            jax.ShapeDtypeStruct((n // seq, ATTN_WIDTH, seq), _F32),
        ],
        compiler_params=_params("parallel"),
        name="proj",
    )(x, g, w_in, b_gate, gq, gk)


Q_ROWS = 2 * MOBA_BLOCK
PREP_ROWS = 4 * MOBA_BLOCK
LOG2E = 1.4426950408889634


def _moba_kernel(q_ref, k_ref, v_ref, o_ref, kaug, vaug, qaug, km, s_buf, mrun, mb, acc_s, *, nb, n_sel):
    step = pl.program_id(2)
    blk = MOBA_BLOCK
    blk_shift = blk.bit_length() - 1
    half = HEAD_DIM
    lane = lax.broadcasted_iota(jnp.int32, (1, LANES), 1)
    in_head = [(lane >= half * h) & (lane < half * (h + 1)) for h in range(2)]

    @pl.when(step == 0)
    def _():
        km[...] = jnp.mean(k_ref[0].reshape(nb, blk, LANES), axis=1)
        km_split = [_split(jnp.where(in_head[h], km[...], 0.0)) for h in range(2)]

        def prep(c, carry):
            r0 = pl.multiple_of(c * PREP_ROWS, PREP_ROWS)
            rows = pl.ds(r0, PREP_ROWS)
            for j in range(PREP_ROWS // blk):
                rows_j = pl.ds(r0 + j * blk, blk)
                for h in range(2):
                    onehot = (lane - half * (1 - h) == c * (PREP_ROWS // blk) + j).astype(_F32)
                    kaug[h, rows_j, :] = jnp.where(in_head[h], k_ref[0, rows_j, :], onehot).astype(_MXU)
            v = v_ref[0, rows, :]
            q2 = q_ref[0, rows, :]
            q_hi, q_lo = _split(q2)
            blk_id = lax.broadcasted_iota(jnp.int32, (nb, PREP_ROWS), 0)
            own = lax.shift_right_logical(
                r0 + lax.broadcasted_iota(jnp.int32, (nb, PREP_ROWS), 1), blk_shift)
            past = blk_id < own
            for h in range(2):
                vaug[h, rows, :] = jnp.where(in_head[h], v, 1.0).astype(_MXU)
                km_hi, km_lo = km_split[h]
                sb = _dot_nt(km_hi, q_hi) + (_dot_nt(km_lo, q_hi) + _dot_nt(km_hi, q_lo))
                cur = jnp.where(past, sb, NEG)
                sel = jnp.zeros(cur.shape, jnp.bool_)
                for _ in range(n_sel):
                    mx = jnp.max(cur, axis=0, keepdims=True)
                    idx = jnp.min(jnp.where(cur == mx, blk_id, nb), axis=0, keepdims=True)
                    pick = blk_id == idx
                    sel = sel | pick
                    cur = jnp.where(pick, -jnp.inf, cur)
                bias_t = jnp.where((sel & past) | (blk_id == own), 0.0, NEG)
                before = half * (1 - h)
                parts = [bias_t, jnp.zeros((LANES - before - nb, PREP_ROWS), _F32)]
                if before:
                    parts.insert(0, jnp.zeros((before, PREP_ROWS), _F32))
                bias_lanes = jnp.concatenate(parts, axis=0).T
                qaug[h, rows, :] = jnp.where(in_head[h], q2 * (SCALE * LOG2E), bias_lanes).astype(_MXU)
            return carry

        lax.fori_loop(0, k_ref.shape[1] // PREP_ROWS, prep, 0)

    def rows_of(ref, h, t, n=1):
        return ref[h, pl.ds(pl.multiple_of(t * Q_ROWS, Q_ROWS), n * Q_ROWS), :]

    def lane_max(s):
        m = s[:, :LANES]
        for c in range(1, s.shape[1] // LANES):
            m = jnp.maximum(m, s[:, c * LANES:(c + 1) * LANES])
        return m

    groups = step + 1
    odd = groups % 2 == 1

    key_ahead = (lax.broadcasted_iota(jnp.int32, (Q_ROWS, 2 * Q_ROWS), 1)
                 - lax.broadcasted_iota(jnp.int32, (Q_ROWS, 2 * Q_ROWS), 0))
    for h in range(2):
        mrun[h] = jnp.full((Q_ROWS, LANES), NEG, _F32)

    def scores(t, n, causal):
        for h in range(2):
            s = _dot_nt(rows_of(qaug, h, step), rows_of(kaug, h, t, n))
            if causal:
                s = jnp.where(key_ahead[:, :n * Q_ROWS] <= (step - t) * Q_ROWS, s, NEG)
            for c in range(n):
                s_buf[h, t + c] = s[:, c * Q_ROWS:(c + 1) * Q_ROWS]
            mrun[h] = jnp.maximum(mrun[h], lane_max(s))

    def scores_quad(tt, carry):
        scores(4 * tt, 4, False)
        return carry

    unmasked = groups - jnp.where(odd, 1, 2)
    lax.fori_loop(0, lax.shift_right_logical(unmasked, 2), scores_quad, 0)

    @pl.when((unmasked & 3) == 2)
    def _():
        scores(unmasked - 2, 2, False)

    @pl.when(odd)
    def _():
        scores(step, 1, True)

    @pl.when(jnp.logical_not(odd))
    def _():
        scores(step - 1, 2, True)

    for h in range(2):
        mb[h] = jnp.broadcast_to(jnp.max(mrun[h], axis=1, keepdims=True), (Q_ROWS, LANES))
        acc_s[h] = jnp.zeros((Q_ROWS, LANES), _F32)

    def attend(t, n):
        for h in range(2):
            s = jnp.concatenate([s_buf[h, t + c] for c in range(n)], axis=1)
            m = jnp.concatenate([mb[h]] * (s.shape[1] // LANES), axis=1)
            acc_s[h] = acc_s[h] + _dot(jnp.exp2(s - m).astype(_MXU), rows_of(vaug, h, t, n))

    def attend_quad(tt, carry):
        attend(4 * tt, 4)
        return carry

    lax.fori_loop(0, lax.shift_right_logical(groups, 2), attend_quad, 0)

    @pl.when((groups & 3) >= 2)
    def _():
        attend(groups & ~3, 2)

    @pl.when(odd)
    def _():
        attend(step, 1)

    num = jnp.where(in_head[0], acc_s[0], acc_s[1])
    den = pltpu.roll(jnp.where(in_head[0], acc_s[1], acc_s[0]), half, axis=1)
    o_ref[0] = num / den


def _moba_prompt(q, k, v):
    b, s, w = q.shape
    assert w == ATTN_WIDTH and s % PREP_ROWS == 0 and MOBA_BLOCK & (MOBA_BLOCK - 1) == 0
    nb = s // MOBA_BLOCK
    assert nb <= HEAD_DIM and nb % 8 == 0
    n_sel = min(MOBA_TOP_K, nb - 1)
    whole = pl.BlockSpec((1, s, LANES), lambda bi, p, i: (bi, 0, p))
    return pl.pallas_call(
        functools.partial(_moba_kernel, nb=nb, n_sel=n_sel),
        grid=(b, w // LANES, s // Q_ROWS),
        in_specs=[whole, whole, whole],
        out_specs=pl.BlockSpec((1, Q_ROWS, LANES), lambda bi, p, i: (bi, i, p)),
        out_shape=jax.ShapeDtypeStruct((b, s, w), _F32),
        scratch_shapes=[
            pltpu.VMEM((2, s, LANES), _MXU),
            pltpu.VMEM((2, s, LANES), _MXU),
            pltpu.VMEM((2, s, LANES), _MXU),
            pltpu.VMEM((nb, LANES), _F32),
            pltpu.VMEM((2, s // Q_ROWS, Q_ROWS, Q_ROWS), _F32),
            pltpu.VMEM((2, Q_ROWS, LANES), _F32),
            pltpu.VMEM((2, Q_ROWS, LANES), _F32),
            pltpu.VMEM((2, Q_ROWS, LANES), _F32),
        ],
        compiler_params=_params("parallel", "parallel", "arbitrary"),
        name="moba_prompt",
    )(q, k, v)


DEC_CHUNK = 8
DEC_SLOTS = 4
GATHER_SLOTS = 3


def _dec_scores_kernel(pt_ref, qt_ref, k_hbm, w_ref, sel_ref, kbuf, qb, sem, *, n_pages, n_sel):
    b = pl.program_id(0)
    n_chunks = n_pages // DEC_CHUNK
    total = pl.num_programs(0) * n_chunks

    def page_copy(g, pg):
        phys = pt_ref[g * DEC_CHUNK + pg]
        slot = g % DEC_SLOTS
        return pltpu.make_async_copy(k_hbm.at[phys], kbuf.at[slot, pg], sem.at[slot])

    def start_chunk(g):
        for pg in range(DEC_CHUNK):
            page_copy(g, pg).start()

    @pl.when(b == 0)
    def _():
        for g in range(DEC_SLOTS - 1):
            start_chunk(g)

    qb[...] = jnp.broadcast_to(_column(qt_ref, b), (ATTN_WIDTH, LANES)).reshape(N_HEADS, HEAD_DIM, LANES)

    def chunk_body(c, carry):
        g = b * n_chunks + c
        nxt = g + DEC_SLOTS - 1

        @pl.when(nxt < total)
        def _():
            start_chunk(nxt)

        for pg in range(DEC_CHUNK):
            page_copy(g, pg).wait()
        slot = g % DEC_SLOTS
        for pg in range(DEC_CHUNK):
            w = jnp.sum(kbuf[slot, pg] * qb[...], axis=1)
            w_ref[0, pl.ds(c * DEC_CHUNK + pg, 1)] = w[None]
        return carry

    lax.fori_loop(0, n_chunks, chunk_body, 0)
    _top_blocks(w_ref, sel_ref, n_sel)


def _dec_attend_kernel(pt_ref, sel_ref, wa_ref, wb_ref, qt_ref, kt_ref, vt_ref, v_hbm, o_ref,
                       vbuf, own_s, coef_s, sem, *, n_pages, n_sel, split):
    b = pl.program_id(0)
    last = pl.num_programs(0) - 1
    page_per_blk = MOBA_BLOCK // v_hbm.shape[3]
    n_gather = n_sel * page_per_blk

    def chosen_page(bb, h, g):
        blk = sel_ref[(bb * n_sel + g // page_per_blk) * N_HEADS + h]
        return blk * page_per_blk + g % page_per_blk

    def copies(bb, slot):
        return [pltpu.make_async_copy(v_hbm.at[pt_ref[bb * n_pages + chosen_page(bb, h, g)], h],
                                      vbuf.at[slot, h, g], sem.at[slot])
                for h in range(N_HEADS) for g in range(n_gather)]

    @pl.when(b == 0)
    def _():
        o_ref[...] = jnp.zeros(o_ref.shape, _F32)
        coef_s[...] = jnp.zeros(coef_s.shape, _F32)
        qk = qt_ref[...] * kt_ref[...]
        own_s[...] = jnp.sum(qk.reshape(N_HEADS, HEAD_DIM, qk.shape[1]), axis=1) * SCALE
        for ahead in range(GATHER_SLOTS - 1):
            for c in copies(ahead, ahead):
                c.start()

    @pl.when(b + GATHER_SLOTS - 1 <= last)
    def _():
        for c in copies(b + GATHER_SLOTS - 1, (b + GATHER_SLOTS - 1) % GATHER_SLOTS):
            c.start()

    slot = b % GATHER_SLOTS
    for h in range(N_HEADS):
        for g in range(n_gather):
            pltpu.make_async_copy(v_hbm.at[0, h], vbuf.at[slot, h, g], sem.at[slot]).wait()

    lane = lax.broadcasted_iota(jnp.int32, (1, o_ref.shape[1]), 1)
    head = lax.broadcasted_iota(jnp.int32, (N_HEADS, 1), 0)
    here = lane == b
    s_own = jnp.sum(jnp.where(here, own_s[...], 0.0), axis=1, keepdims=True)
    tiles = []
    for g in range(n_gather):
        t = jnp.zeros((N_HEADS, wa_ref.shape[3]), _F32)
        for h in range(N_HEADS):
            pg = pl.ds(chosen_page(b, h, g), 1)
            t = jnp.where(head == h, jnp.where(b < split, wa_ref[0, pg][0], wb_ref[0, pg][0]), t)
        tiles.append(t * SCALE)
    m = s_own
    for t in tiles:
        m = jnp.maximum(m, jnp.max(t, axis=1, keepdims=True))
    p_own = jnp.exp(s_own - m)
    denom = p_own
    probs = []
    for t in tiles:
        p = jnp.exp(t - m)
        denom = denom + jnp.sum(p, axis=1, keepdims=True)
        probs.append(p)
    inv = 1.0 / denom
    for h in range(N_HEADS):
        acc = jnp.zeros((HEAD_DIM, probs[0].shape[1]), _F32)
        for g in range(n_gather):
            acc = acc + probs[g][h:h + 1, :] * vbuf[slot, h, g]
        hs = slice(h * HEAD_DIM, (h + 1) * HEAD_DIM)
        o_ref[hs, :] = jnp.where(here, jnp.sum(acc, axis=1, keepdims=True) * inv[h:h + 1, :], o_ref[hs, :])
    coef_s[...] = jnp.where(here, p_own * inv, coef_s[...])

    @pl.when(b == last)
    def _():
        for h in range(N_HEADS):
            hs = slice(h * HEAD_DIM, (h + 1) * HEAD_DIM)
            o_ref[hs, :] = o_ref[hs, :] + coef_s[h:h + 1, :] * vt_ref[hs, :]


def _decode_dims(page_table, cache_kt):
    db, n_pages = page_table.shape
    page = cache_kt.shape[3]
    assert MOBA_BLOCK % page == 0 and (n_pages * page) % MOBA_BLOCK == 0
    n_sel = min(MOBA_TOP_K, n_pages * page // MOBA_BLOCK)
    assert n_sel > 0
    return db, n_pages, page, n_sel


def _decode_scores(qt, cache_kt, page_table):
    db, n_pages, page, n_sel = _decode_dims(page_table, cache_kt)
    assert n_pages % DEC_CHUNK == 0 and db * (n_pages // DEC_CHUNK) >= DEC_SLOTS
    return pl.pallas_call(
        functools.partial(_dec_scores_kernel, n_pages=n_pages, n_sel=n_sel),
        grid_spec=pltpu.PrefetchScalarGridSpec(
            num_scalar_prefetch=1,
            grid=(db,),
            in_specs=[pl.BlockSpec(qt.shape, lambda i, *_: (0, 0)), pl.BlockSpec(memory_space=pl.ANY)],
            out_specs=[
                pl.BlockSpec((1, n_pages, N_HEADS, page), lambda i, *_: (i, 0, 0, 0)),
                pl.BlockSpec((1, n_sel, N_HEADS, LANES), lambda i, *_: (i, 0, 0, 0)),
            ],
            scratch_shapes=[
                pltpu.VMEM((DEC_SLOTS, DEC_CHUNK, N_HEADS, HEAD_DIM, page), _F32),
                pltpu.VMEM((N_HEADS, HEAD_DIM, LANES), _F32),
                pltpu.SemaphoreType.DMA((DEC_SLOTS,)),
            ],
        ),
        out_shape=[
            jax.ShapeDtypeStruct((db, n_pages, N_HEADS, page), _F32),
            jax.ShapeDtypeStruct((db, n_sel, N_HEADS, LANES), jnp.int32),
        ],
        compiler_params=_params("arbitrary"),
        name="decode_scores",
    )(page_table.reshape(-1), qt, cache_kt)


def _decode_attend(logits_a, logits_b, sel, qt, kt, vt, cache_vt, page_table):
    db, n_pages, page, n_sel = _decode_dims(page_table, cache_vt)
    split = logits_a.shape[0]
    assert db >= GATHER_SLOTS
    whole = lambda a: pl.BlockSpec(a.shape, lambda i, *_: (0,) * a.ndim)
    n_gather = n_sel * (MOBA_BLOCK // page)
    out_t = pl.pallas_call(
        functools.partial(_dec_attend_kernel, n_pages=n_pages, n_sel=n_sel, split=split),
        grid_spec=pltpu.PrefetchScalarGridSpec(
            num_scalar_prefetch=2,
            grid=(db,),
            in_specs=[
                pl.BlockSpec((1, n_pages, N_HEADS, page), lambda i, *_: (jnp.minimum(i, split - 1), 0, 0, 0)),
                pl.BlockSpec((1, n_pages, N_HEADS, page), lambda i, *_: (jnp.maximum(i - split, 0), 0, 0, 0)),
                whole(qt), whole(kt), whole(vt), pl.BlockSpec(memory_space=pl.ANY),
            ],
            out_specs=pl.BlockSpec((ATTN_WIDTH, db), lambda i, *_: (0, 0)),
            scratch_shapes=[
                pltpu.VMEM((GATHER_SLOTS, N_HEADS, n_gather, HEAD_DIM, page), _F32),
                pltpu.VMEM((N_HEADS, db), _F32),
                pltpu.VMEM((N_HEADS, db), _F32),
                pltpu.SemaphoreType.DMA((GATHER_SLOTS,)),
            ],
        ),
        out_shape=jax.ShapeDtypeStruct((ATTN_WIDTH, db), _F32),
        compiler_params=_params("arbitrary"),
        name="decode_attend",
    )(page_table.reshape(-1), sel[:, :, :, 0].reshape(-1), logits_a, logits_b, qt, kt, vt, cache_vt)
    return out_t.T


def _pool_branch(d_groups, wpg_ref, sp_ref):
    outs = []
    for g, d in enumerate(d_groups):
        cols = slice(g * POOL_GROUP_WIDTH, (g + 1) * POOL_GROUP_WIDTH)
        outs.append(_dot(d.astype(_MXU), wpg_ref[g]) * sp_ref[:, cols])
    return jnp.concatenate(outs, axis=1)


def _mix_tail(x, pool, attn_proj, gt_ref, wbp_ref, wo_ref, o_ref):
    d = x.shape[1]
    m = gt_ref[:, :d] * _dot(pool.astype(_MXU), wbp_ref[...]) + gt_ref[:, d:] * attn_proj
    o_ref[...] = x + _dot(m.astype(_MXU), wo_ref[...])


def _mix_prompt_kernel(x_ref, at_ref, u_ref, hist_ref, gt_ref, wpg_ref, sp_ref, wbp_ref, wba_ref,
                       wo_ref, o_ref, ext, lvl, *, seq):
    tm = u_ref.shape[0]
    pad = 8
    n = tm + HIST_ROWS
    pos0 = (pl.program_id(0) * tm) % seq
    ext[0:pad, :] = jnp.zeros((pad, POOL_WIDTH), _F32)
    ext[pad:pad + HIST_ROWS, :] = jnp.where(pos0 > 0, hist_ref[...], 0.0)
    ext[pad + HIST_ROWS:, :] = u_ref[...]
    lvl[:, 0:pad, :] = jnp.zeros((2, pad, POOL_GROUP_WIDTH), _F32)
    attn_proj = _dot(at_ref[...].astype(_MXU), wba_ref[...])
    bits = pltpu.bitcast(attn_proj[tm - pad:, attn_proj.shape[1] - POOL_GROUP_WIDTH:], jnp.uint32)
    tie_cols = slice(POOL_WIDTH - POOL_GROUP_WIDTH, POOL_WIDTH)
    ext[0:pad, tie_cols] = ((bits >> 16) >> 16).astype(_F32)
    pos = pos0 + lax.broadcasted_iota(jnp.int32, (tm, 1), 0)
    d_groups = []
    for g, w in enumerate(POOL_WINDOWS):
        assert w & (w - 1) == 0 and w <= HIST_ROWS
        cols = slice(g * POOL_GROUP_WIDTH, (g + 1) * POOL_GROUP_WIDTH)
        read = lambda start: ext[pl.ds(start, n), cols]
        shift = 1
        while True:
            total = read(pad) + read(pad - shift)
            shift *= 2
            if shift == w:
                break
            slot = lvl.at[(shift.bit_length()) % 2]
            slot[pl.ds(pad, n), :] = total
            read = lambda start, slot=slot: slot[pl.ds(start, n), :]
        cur = ext[pad + HIST_ROWS:, cols]
        cnt = jnp.minimum(w, pos + 1).astype(_F32)
        d_groups.append(total[HIST_ROWS:] / cnt - cur)
    pool = _pool_branch(d_groups, wpg_ref, sp_ref)
    _mix_tail(x_ref[...], pool, attn_proj, gt_ref, wbp_ref, wo_ref, o_ref)


def _mix_decode_kernel(x_ref, at_ref, u_ref, hist_ref, gt_ref, wpg_ref, sp_ref, wbp_ref, wba_ref,
                       wo_ref, o_ref, *, past):
    d_groups = []
    for g, w in enumerate(POOL_WINDOWS):
        cols = slice(g * POOL_GROUP_WIDTH, (g + 1) * POOL_GROUP_WIDTH)
        cur = u_ref[:, cols]
        total = cur
        for j in range(1, w):
            total = total + hist_ref[POOL_HIST - j, :, cols]
        d_groups.append(total / float(min(w, past + 1)) - cur)
    pool = _pool_branch(d_groups, wpg_ref, sp_ref)
    attn_proj = _dot(at_ref[...].astype(_MXU), wba_ref[...])
    _mix_tail(x_ref[...], pool, attn_proj, gt_ref, wbp_ref, wo_ref, o_ref)


def _mix(x, attn, u, hist, gates, wpg, sp, wbp, wba, wo, *, seq=None, past=None):
    n, d = x.shape
    tm = _token_tile(n)
    row = lambda w: pl.BlockSpec((tm, w), lambda i: (i, 0))
    if seq is not None:
        assert seq % tm == 0 and tm % HIST_ROWS == 0
        body = functools.partial(_mix_prompt_kernel, seq=seq)
        hist_spec = pl.BlockSpec(
            (HIST_ROWS, POOL_WIDTH), lambda i: (jnp.maximum(i * (tm // HIST_ROWS) - 1, 0), 0))
        scratch = [pltpu.VMEM((8 + HIST_ROWS + tm, POOL_WIDTH), _F32),
                   pltpu.VMEM((2, 8 + HIST_ROWS + tm, POOL_GROUP_WIDTH), _F32)]
    else:
        body = functools.partial(_mix_decode_kernel, past=past)
        hist_spec = pl.BlockSpec((POOL_HIST, tm, POOL_WIDTH), lambda i: (0, i, 0))
        scratch = []
    return pl.pallas_call(
        body,
        grid=(n // tm,),
        in_specs=[
            row(d), row(ATTN_WIDTH), row(POOL_WIDTH), hist_spec, row(2 * d),
            _resident(wpg.shape), _resident(sp.shape), _resident(wbp.shape),
            _resident(wba.shape), _resident(wo.shape),
        ],
        out_specs=row(d),
        out_shape=jax.ShapeDtypeStruct((n, d), _F32),
        scratch_shapes=scratch,
        compiler_params=_params("parallel"),
        name="mix",
    )(x, attn, u, hist, gates, wpg, sp, wbp, wba, wo)


def kernel(x_prompt, x_sample, cache_k, cache_v, state_pool, page_table, g_ffn1, w1_ffn1, w3_ffn1,
           w2_ffn1, g_mix, w_in, b_gate, g_q, g_k, w_pool_grp, s_pool, w_branch_pool,
           w_branch_attn, w_out, g_ffn2, w1_ffn2, w3_ffn2, w2_ffn2):
    assert w_in.shape[0] == 1 and x_sample.shape[1] == 1
    b, s, d = x_prompt.shape
    db = x_sample.shape[0]
    page = cache_k.shape[2]
    past = page_table.shape[1] * page
    mx = lambda w: w[0].astype(_MXU)

    ffn1 = (g_ffn1, mx(w1_ffn1), mx(w3_ffn1), mx(w2_ffn1))
    ffn2 = (g_ffn2, mx(w1_ffn2), mx(w3_ffn2), mx(w2_ffn2))
    head_gain = lambda g: jnp.tile(g, (1, N_HEADS)).reshape(ATTN_WIDTH, 1)
    proj = (g_mix, mx(w_in), b_gate, head_gain(g_q), head_gain(g_k))
    post = (mx(w_pool_grp), s_pool, mx(w_branch_pool), mx(w_branch_attn), mx(w_out))

    xs = _ffn(x_sample.reshape(db, d), *ffn1)
    us, qs, _, _, gates_s, kst, vst = _proj(xs, *proj, seq=db)
    qst = qs.T
    cache_kt = jnp.transpose(cache_k[0], (0, 2, 3, 1))
    cache_vt = jnp.transpose(cache_v[0], (0, 2, 3, 1))

    steps = (b * s) // _token_tile(b * s)
    ride = db == 2 * steps
    if ride:
        _, n_pages, _, n_sel = _decode_dims(page_table, cache_kt)
        rider = lambda base: dict(pt=page_table.reshape(-1), qt=qst, cache_kt=cache_kt, base=base,
                                  n_pages=n_pages, n_sel=n_sel)
        xp, logits_a, sel_a = _ffn_scores(x_prompt.reshape(b * s, d), *ffn1, **rider(0))
    else:
        xp = _ffn(x_prompt.reshape(b * s, d), *ffn1)
    u, q, k, v, gates, kt, vt = _proj(xp, *proj, seq=s)
    shape3 = (b, s, ATTN_WIDTH)
    attn = _moba_prompt(q.reshape(shape3), k.reshape(shape3), v.reshape(shape3))
    xp = _mix(xp, attn.reshape(b * s, ATTN_WIDTH), u, u, gates, *post, seq=s)
    if ride:
        yp, logits_b, sel_b = _ffn_scores(xp, *ffn2, **rider(steps))
        sel = jnp.concatenate([sel_a, sel_b], axis=0)
    else:
        yp = _ffn(xp, *ffn2)
        logits_a, sel = _decode_scores(qst, cache_kt, page_table)
        logits_b = logits_a
    yp = yp.reshape(b, s, d)

    attn_s = _decode_attend(logits_a, logits_b, sel, qst, kst[0], vst[0], cache_vt, page_table)
    hist = jnp.transpose(state_pool[0], (1, 0, 2))
    xs = _mix(xs, attn_s, us, hist, gates_s, *post, past=past)
    ys = _ffn(xs, *ffn2).reshape(db, 1, d)

    cache_p = lambda t: jnp.transpose(t.reshape(1, b, N_HEADS, HEAD_DIM, s), (0, 1, 4, 2, 3))
    cache_s = lambda t: jnp.transpose(t.reshape(1, 1, N_HEADS, HEAD_DIM, db), (0, 4, 1, 2, 3))
    pool_p = u.reshape(b, s, POOL_WIDTH)[:, s - POOL_HIST:]
    pool_s = jnp.concatenate([state_pool[0, :, 1:], us[:, None]], axis=1)
    return (yp, ys, cache_p(kt), cache_p(vt), pool_p[None], cache_s(kst), cache_s(vst), pool_s[None])
```

```python
import functools

import jax
import jax.numpy as jnp
from jax import lax
from jax.experimental import pallas as pl
from jax.experimental.pallas import tpu as pltpu

N_HEADS = 8
HEAD_DIM = 64
ATTN_WIDTH = N_HEADS * HEAD_DIM
MOBA_BLOCK = 256
MOBA_TOP_K = 3
POOL_WINDOWS = (2, 4, 8, 16)
POOL_GROUP_WIDTH = 128
POOL_WIDTH = len(POOL_WINDOWS) * POOL_GROUP_WIDTH
POOL_HIST = max(POOL_WINDOWS) - 1
HIST_ROWS = 16
EPS = 1e-6
NEG = -1e30
SCALE = HEAD_DIM ** -0.5

LANES = 128
MXU_WIDTH = 256
TOKEN_TILE = 512
RIDE_CHUNKS = 4
VMEM_LIMIT = 56 * 1024 * 1024

_MXU = jnp.bfloat16
_F32 = jnp.float32


def _dot(a, b):
    return jnp.dot(a, b, preferred_element_type=_F32)


def _dot_nt(a, b):
    return lax.dot_general(a, b, (((1,), (1,)), ((), ())), preferred_element_type=_F32)


def _split(a):
    hi = a.astype(_MXU)
    lo = (a - hi.astype(_F32)).astype(_MXU)
    return hi, lo


def _rms(x, g):
    ms = jnp.mean(x * x, axis=-1, keepdims=True)
    return x * lax.rsqrt(ms + EPS) * g


def _params(*sem):
    return pltpu.CompilerParams(dimension_semantics=sem, vmem_limit_bytes=VMEM_LIMIT)


def _resident(shape):
    return pl.BlockSpec(shape, lambda *_: (0,) * len(shape), pipeline_mode=pl.Buffered(1))


def _token_tile(n):
    return TOKEN_TILE if n % TOKEN_TILE == 0 else n


def _column(ref, b):
    lane = lax.broadcasted_iota(jnp.int32, (1, ref.shape[1]), 1)
    return jnp.sum(jnp.where(lane == b, ref[...], 0.0), axis=1, keepdims=True)


def _top_blocks(w_ref, sel_ref, n_sel):
    _, n_pages, heads, page = w_ref.shape
    page_per_blk = MOBA_BLOCK // page
    nb = n_pages // page_per_blk
    wb = jnp.sum(w_ref[0].reshape(nb, page_per_blk, heads, page), axis=1)
    cur = jnp.sum(wb, axis=2, keepdims=True) * (1.0 / MOBA_BLOCK)
    blk_id = lax.broadcasted_iota(jnp.int32, cur.shape, 0)
    for r in range(n_sel):
        mx = jnp.max(cur, axis=0, keepdims=True)
        idx = jnp.min(jnp.where(cur == mx, blk_id, nb), axis=0, keepdims=True)
        sel_ref[0, r] = jnp.broadcast_to(idx[0], (heads, LANES))
        cur = jnp.where(blk_id == idx, -jnp.inf, cur)


def _ffn_kernel(x_ref, g_ref, w1_ref, w3_ref, w2_ref, o_ref):
    x = x_ref[...]
    h = _rms(x, g_ref[...]).astype(_MXU)
    a = _dot(h, w1_ref[...])
    b = _dot(h, w3_ref[...])
    act = (a * jax.nn.sigmoid(a) * b).astype(_MXU)
    o_ref[...] = x + 0.5 * _dot(act, w2_ref[...])


def _ffn_scores_kernel(pt_ref, x_ref, g_ref, w1_ref, w3_ref, w2_ref, qt_ref, k_hbm,
                       o_ref, w_ref, sel_ref, kbuf, qb, sem, *, base, n_pages, n_sel):
    step = pl.program_id(0)
    b = base + step
    per = n_pages // RIDE_CHUNKS

    def page_copy(bb, pg):
        return pltpu.make_async_copy(k_hbm.at[pt_ref[bb * n_pages + pg]], kbuf.at[pg], sem.at[0])

    @pl.when(step == 0)
    def _():
        for pg in range(n_pages):
            page_copy(b, pg).start()

    x = x_ref[...]
    h = _rms(x, g_ref[...]).astype(_MXU)
    a = _dot(h, w1_ref[...])
    gate = _dot(h, w3_ref[...])
    act = (a * jax.nn.sigmoid(a) * gate).astype(_MXU)

    for pg in range(n_pages):
        page_copy(b, pg).wait()
    qb[...] = jnp.broadcast_to(_column(qt_ref, b), (ATTN_WIDTH, LANES)).reshape(N_HEADS, HEAD_DIM, LANES)
    f = act.shape[1]
    tile = MXU_WIDTH if f >= RIDE_CHUNKS * MXU_WIDTH else LANES
    bounds = [min(f, -(-(f * c) // (RIDE_CHUNKS * tile)) * tile) for c in range(RIDE_CHUNKS)] + [f]
    acc = x
    for c in range(RIDE_CHUNKS):
        seen = jnp.zeros((1, LANES), _F32)
        for hd in range(N_HEADS):
            qh = qb[hd]
            for pg in range(c * per, (c + 1) * per):
                row = jnp.sum(kbuf[pg, hd] * qh, axis=0, keepdims=True)
                w_ref[0, pg, hd:hd + 1, :] = row
                seen = seen + row
        bits = pltpu.bitcast(seen, jnp.uint32)
        half = 0.5 + ((bits >> 16) >> 16).astype(_F32)
        rows = slice(bounds[c], bounds[c + 1])
        part = _dot(act[:, rows], w2_ref[rows, :])
        acc = acc + part * jnp.concatenate([half] * (x.shape[1] // LANES), axis=1)
    o_ref[...] = acc
    _top_blocks(w_ref, sel_ref, n_sel)

    @pl.when(step + 1 < pl.num_programs(0))
    def _():
        for pg in range(n_pages):
            page_copy(b + 1, pg).start()


def _ffn_scores(x, g, w1, w3, w2, pt, qt, cache_kt, *, base, n_pages, n_sel):
    n, d = x.shape
    f = w1.shape[1]
    tm = _token_tile(n)
    steps = n // tm
    page = cache_kt.shape[3]
    assert n_pages % RIDE_CHUNKS == 0 and f % LANES == 0 and f >= RIDE_CHUNKS * LANES
    fixed = lambda shape: pl.BlockSpec(shape, lambda i, *_: (0,) * len(shape), pipeline_mode=pl.Buffered(1))
    return pl.pallas_call(
        functools.partial(_ffn_scores_kernel, base=base, n_pages=n_pages, n_sel=n_sel),
        grid_spec=pltpu.PrefetchScalarGridSpec(
            num_scalar_prefetch=1,
            grid=(steps,),
            in_specs=[
                pl.BlockSpec((tm, d), lambda i, *_: (i, 0)),
                fixed((1, d)), fixed((d, f)), fixed((d, f)), fixed((f, d)), fixed(qt.shape),
                pl.BlockSpec(memory_space=pl.ANY),
            ],
            out_specs=[
                pl.BlockSpec((tm, d), lambda i, *_: (i, 0)),
                pl.BlockSpec((1, n_pages, N_HEADS, page), lambda i, *_: (i, 0, 0, 0)),
                pl.BlockSpec((1, n_sel, N_HEADS, LANES), lambda i, *_: (i, 0, 0, 0)),
            ],
            scratch_shapes=[
                pltpu.VMEM((n_pages, N_HEADS, HEAD_DIM, page), _F32),
                pltpu.VMEM((N_HEADS, HEAD_DIM, LANES), _F32),
                pltpu.SemaphoreType.DMA((1,)),
            ],
        ),
        out_shape=[
            jax.ShapeDtypeStruct((n, d), _F32),
            jax.ShapeDtypeStruct((steps, n_pages, N_HEADS, page), _F32),
            jax.ShapeDtypeStruct((steps, n_sel, N_HEADS, LANES), jnp.int32),
        ],
        compiler_params=_params("arbitrary"),
        name="ffn_scores",
    )(pt, x, g, w1, w3, w2, qt, cache_kt)


def _ffn(x, g, w1, w3, w2):
    n, d = x.shape
    f = w1.shape[1]
    tm = _token_tile(n)
    return pl.pallas_call(
        _ffn_kernel,
        grid=(n // tm,),
        in_specs=[
            pl.BlockSpec((tm, d), lambda i: (i, 0)),
            _resident((1, d)),
            _resident((d, f)),
            _resident((d, f)),
            _resident((f, d)),
        ],
        out_specs=pl.BlockSpec((tm, d), lambda i: (i, 0)),
        out_shape=jax.ShapeDtypeStruct((n, d), _F32),
        compiler_params=_params("parallel"),
        name="ffn",
    )(x, g, w1, w3, w2)


def _head_rms_t(z, g_col):
    zt = z.T
    z3 = zt.reshape(N_HEADS, HEAD_DIM, zt.shape[1])
    ms = jnp.mean(z3 * z3, axis=1, keepdims=True)
    return (z3 * lax.rsqrt(ms + EPS)).reshape(zt.shape) * g_col


def _proj_kernel(x_ref, g_ref, w_ref, b_ref, gq_ref, gk_ref,
                 u_ref, q_ref, k_ref, v_ref, gt_ref, kt_ref, vt_ref):
    h = _rms(x_ref[...], g_ref[...]).astype(_MXU)
    a = ATTN_WIDTH
    o = POOL_WIDTH
    u_ref[...] = _dot(h, w_ref[:, 0:o])
    q_ref[...] = _head_rms_t(_dot(h, w_ref[:, o:o + a]), gq_ref[...]).T
    kt = _head_rms_t(_dot(h, w_ref[:, o + a:o + 2 * a]), gk_ref[...])
    v = _dot(h, w_ref[:, o + 2 * a:o + 3 * a])
    kt_ref[0] = kt
    k_ref[...] = kt.T
    v_ref[...] = v
    vt_ref[0] = v.T
    gt_ref[...] = jax.nn.sigmoid(_dot(h, w_ref[:, o + 3 * a:]) + b_ref[...]).astype(gt_ref.dtype)


def _proj(x, g, w_in, b_gate, gq, gk, *, seq):
    n, d = x.shape
    wtot = w_in.shape[1]
    tm = _token_tile(n)
    assert seq % tm == 0 and n % seq == 0
    tiles_per_seq = seq // tm
    row = lambda w: pl.BlockSpec((tm, w), lambda i: (i, 0))
    tok_minor = pl.BlockSpec((1, ATTN_WIDTH, tm), lambda i: (i // tiles_per_seq, 0, i % tiles_per_seq))
    return pl.pallas_call(
        _proj_kernel,
        grid=(n // tm,),
        in_specs=[
            row(d),
            _resident((1, d)),
            _resident((d, wtot)),
            _resident((1, 2 * d)),
            _resident((ATTN_WIDTH, 1)),
            _resident((ATTN_WIDTH, 1)),
        ],
        out_specs=[row(POOL_WIDTH), row(ATTN_WIDTH), row(ATTN_WIDTH), row(ATTN_WIDTH), row(2 * d),
                   tok_minor, tok_minor],
        out_shape=[
            jax.ShapeDtypeStruct((n, POOL_WIDTH), _F32),
            jax.ShapeDtypeStruct((n, ATTN_WIDTH), _F32),
            jax.ShapeDtypeStruct((n, ATTN_WIDTH), _F32),
            jax.ShapeDtypeStruct((n, ATTN_WIDTH), _F32),
            jax.ShapeDtypeStruct((n, 2 * d), _MXU),
            jax.ShapeDtypeStruct((n // seq, ATTN_WIDTH, seq), _F32),
            jax.ShapeDtypeStruct((n // seq, ATTN_WIDTH, seq), _F32),
        ],
        compiler_params=_params("parallel"),
        name="proj",
    )(x, g, w_in, b_gate, gq, gk)


Q_ROWS = 2 * MOBA_BLOCK
PREP_ROWS = 4 * MOBA_BLOCK
LOG2E = 1.4426950408889634


def _moba_kernel(q_ref, k_ref, v_ref, o_ref, kaug, vaug, qaug, km, s_buf, mrun, mb, acc_s, *, nb, n_sel):
    step = pl.program_id(2)
    blk = MOBA_BLOCK
    blk_shift = blk.bit_length() - 1
    half = HEAD_DIM
    lane = lax.broadcasted_iota(jnp.int32, (1, LANES), 1)
    in_head = [(lane >= half * h) & (lane < half * (h + 1)) for h in range(2)]

    @pl.when(step == 0)
    def _():
        km[...] = jnp.mean(k_ref[0].reshape(nb, blk, LANES), axis=1)
        km_split = [_split(jnp.where(in_head[h], km[...], 0.0)) for h in range(2)]

        def prep(c, carry):
            r0 = pl.multiple_of(c * PREP_ROWS, PREP_ROWS)
            rows = pl.ds(r0, PREP_ROWS)
            for j in range(PREP_ROWS // blk):
                rows_j = pl.ds(r0 + j * blk, blk)
                for h in range(2):
                    onehot = (lane - half * (1 - h) == c * (PREP_ROWS // blk) + j).astype(_F32)
                    kaug[h, rows_j, :] = jnp.where(in_head[h], k_ref[0, rows_j, :], onehot).astype(_MXU)
            v = v_ref[0, rows, :]
            q2 = q_ref[0, rows, :]
            q_hi, q_lo = _split(q2)
            blk_id = lax.broadcasted_iota(jnp.int32, (nb, PREP_ROWS), 0)
            own = lax.shift_right_logical(
                r0 + lax.broadcasted_iota(jnp.int32, (nb, PREP_ROWS), 1), blk_shift)
            past = blk_id < own
            for h in range(2):
                vaug[h, rows, :] = jnp.where(in_head[h], v, 1.0).astype(_MXU)
                km_hi, km_lo = km_split[h]
                sb = _dot_nt(km_hi, q_hi) + (_dot_nt(km_lo, q_hi) + _dot_nt(km_hi, q_lo))
                cur = jnp.where(past, sb, NEG)
                sel = jnp.zeros(cur.shape, jnp.bool_)
                for _ in range(n_sel):
                    mx = jnp.max(cur, axis=0, keepdims=True)
                    idx = jnp.min(jnp.where(cur == mx, blk_id, nb), axis=0, keepdims=True)
                    pick = blk_id == idx
                    sel = sel | pick
                    cur = jnp.where(pick, -jnp.inf, cur)
                bias_t = jnp.where((sel & past) | (blk_id == own), 0.0, NEG)
                before = half * (1 - h)
                parts = [bias_t, jnp.zeros((LANES - before - nb, PREP_ROWS), _F32)]
                if before:
                    parts.insert(0, jnp.zeros((before, PREP_ROWS), _F32))
                bias_lanes = jnp.concatenate(parts, axis=0).T
                qaug[h, rows, :] = jnp.where(in_head[h], q2 * (SCALE * LOG2E), bias_lanes).astype(_MXU)
            return carry

        lax.fori_loop(0, k_ref.shape[1] // PREP_ROWS, prep, 0)

    def rows_of(ref, h, t, n=1):
        return ref[h, pl.ds(pl.multiple_of(t * Q_ROWS, Q_ROWS), n * Q_ROWS), :]

    def lane_max(s):
        m = s[:, :LANES]
        for c in range(1, s.shape[1] // LANES):
            m = jnp.maximum(m, s[:, c * LANES:(c + 1) * LANES])
        return m

    groups = step + 1
    odd = groups % 2 == 1

    key_ahead = (lax.broadcasted_iota(jnp.int32, (Q_ROWS, 2 * Q_ROWS), 1)
                 - lax.broadcasted_iota(jnp.int32, (Q_ROWS, 2 * Q_ROWS), 0))
    for h in range(2):
        mrun[h] = jnp.full((Q_ROWS, LANES), NEG, _F32)

    def scores(t, n, causal):
        for h in range(2):
            s = _dot_nt(rows_of(qaug, h, step), rows_of(kaug, h, t, n))
            if causal:
                s = jnp.where(key_ahead[:, :n * Q_ROWS] <= (step - t) * Q_ROWS, s, NEG)
            for c in range(n):
                s_buf[h, t + c] = s[:, c * Q_ROWS:(c + 1) * Q_ROWS]
            mrun[h] = jnp.maximum(mrun[h], lane_max(s))

    def scores_quad(tt, carry):
        scores(4 * tt, 4, False)
        return carry

    unmasked = groups - jnp.where(odd, 1, 2)
    lax.fori_loop(0, lax.shift_right_logical(unmasked, 2), scores_quad, 0)

    @pl.when((unmasked & 3) == 2)
    def _():
        scores(unmasked - 2, 2, False)

    @pl.when(odd)
    def _():
        scores(step, 1, True)

    @pl.when(jnp.logical_not(odd))
    def _():
        scores(step - 1, 2, True)

    for h in range(2):
        mb[h] = jnp.broadcast_to(jnp.max(mrun[h], axis=1, keepdims=True), (Q_ROWS, LANES))
        acc_s[h] = jnp.zeros((Q_ROWS, LANES), _F32)

    def attend(t, n):
        for h in range(2):
            s = jnp.concatenate([s_buf[h, t + c] for c in range(n)], axis=1)
            m = jnp.concatenate([mb[h]] * (s.shape[1] // LANES), axis=1)
            acc_s[h] = acc_s[h] + _dot(jnp.exp2(s - m).astype(_MXU), rows_of(vaug, h, t, n))

    def attend_quad(tt, carry):
        attend(4 * tt, 4)
        return carry

    lax.fori_loop(0, lax.shift_right_logical(groups, 2), attend_quad, 0)

    @pl.when((groups & 3) >= 2)
    def _():
        attend(groups & ~3, 2)

    @pl.when(odd)
    def _():
        attend(step, 1)

    num = jnp.where(in_head[0], acc_s[0], acc_s[1])
    den = pltpu.roll(jnp.where(in_head[0], acc_s[1], acc_s[0]), half, axis=1)
    o_ref[0] = num / den


def _moba_prompt(q, k, v):
    b, s, w = q.shape
    assert w == ATTN_WIDTH and s % PREP_ROWS == 0 and MOBA_BLOCK & (MOBA_BLOCK - 1) == 0
    nb = s // MOBA_BLOCK
    assert nb <= HEAD_DIM and nb % 8 == 0
    n_sel = min(MOBA_TOP_K, nb - 1)
    whole = pl.BlockSpec((1, s, LANES), lambda bi, p, i: (bi, 0, p))
    return pl.pallas_call(
        functools.partial(_moba_kernel, nb=nb, n_sel=n_sel),
        grid=(b, w // LANES, s // Q_ROWS),
        in_specs=[whole, whole, whole],
        out_specs=pl.BlockSpec((1, Q_ROWS, LANES), lambda bi, p, i: (bi, i, p)),
        out_shape=jax.ShapeDtypeStruct((b, s, w), _F32),
        scratch_shapes=[
            pltpu.VMEM((2, s, LANES), _MXU),
            pltpu.VMEM((2, s, LANES), _MXU),
            pltpu.VMEM((2, s, LANES), _MXU),
            pltpu.VMEM((nb, LANES), _F32),
            pltpu.VMEM((2, s // Q_ROWS, Q_ROWS, Q_ROWS), _F32),
            pltpu.VMEM((2, Q_ROWS, LANES), _F32),
            pltpu.VMEM((2, Q_ROWS, LANES), _F32),
            pltpu.VMEM((2, Q_ROWS, LANES), _F32),
        ],
        compiler_params=_params("parallel", "parallel", "arbitrary"),
        name="moba_prompt",
    )(q, k, v)


DEC_CHUNK = 8
DEC_SLOTS = 4
GATHER_SLOTS = 3


def _dec_scores_kernel(pt_ref, qt_ref, k_hbm, w_ref, sel_ref, kbuf, qb, sem, *, n_pages, n_sel):
    b = pl.program_id(0)
    n_chunks = n_pages // DEC_CHUNK
    total = pl.num_programs(0) * n_chunks

    def page_copy(g, pg):
        phys = pt_ref[g * DEC_CHUNK + pg]
        slot = g % DEC_SLOTS
        return pltpu.make_async_copy(k_hbm.at[phys], kbuf.at[slot, pg], sem.at[slot])

    def start_chunk(g):
        for pg in range(DEC_CHUNK):
            page_copy(g, pg).start()

    @pl.when(b == 0)
    def _():
        for g in range(DEC_SLOTS - 1):
            start_chunk(g)

    qb[...] = jnp.broadcast_to(_column(qt_ref, b), (ATTN_WIDTH, LANES)).reshape(N_HEADS, HEAD_DIM, LANES)

    def chunk_body(c, carry):
        g = b * n_chunks + c
        nxt = g + DEC_SLOTS - 1

        @pl.when(nxt < total)
        def _():
            start_chunk(nxt)

        for pg in range(DEC_CHUNK):
            page_copy(g, pg).wait()
        slot = g % DEC_SLOTS
        for pg in range(DEC_CHUNK):
            w = jnp.sum(kbuf[slot, pg] * qb[...], axis=1)
            w_ref[0, pl.ds(c * DEC_CHUNK + pg, 1)] = w[None]
        return carry

    lax.fori_loop(0, n_chunks, chunk_body, 0)
    _top_blocks(w_ref, sel_ref, n_sel)


def _dec_attend_kernel(pt_ref, sel_ref, wa_ref, wb_ref, qt_ref, kt_ref, vt_ref, v_hbm, o_ref,
                       vbuf, own_s, coef_s, sem, *, n_pages, n_sel, split):
    b = pl.program_id(0)
    last = pl.num_programs(0) - 1
    page_per_blk = MOBA_BLOCK // v_hbm.shape[3]
    n_gather = n_sel * page_per_blk

    def chosen_page(bb, h, g):
        blk = sel_ref[(bb * n_sel + g // page_per_blk) * N_HEADS + h]
        return blk * page_per_blk + g % page_per_blk

    def copies(bb, slot):
        return [pltpu.make_async_copy(v_hbm.at[pt_ref[bb * n_pages + chosen_page(bb, h, g)], h],
                                      vbuf.at[slot, h, g], sem.at[slot])
                for h in range(N_HEADS) for g in range(n_gather)]

    @pl.when(b == 0)
    def _():
        o_ref[...] = jnp.zeros(o_ref.shape, _F32)
        coef_s[...] = jnp.zeros(coef_s.shape, _F32)
        qk = qt_ref[...] * kt_ref[...]
        own_s[...] = jnp.sum(qk.reshape(N_HEADS, HEAD_DIM, qk.shape[1]), axis=1) * SCALE
        for ahead in range(GATHER_SLOTS - 1):
            for c in copies(ahead, ahead):
                c.start()

    @pl.when(b + GATHER_SLOTS - 1 <= last)
    def _():
        for c in copies(b + GATHER_SLOTS - 1, (b + GATHER_SLOTS - 1) % GATHER_SLOTS):
            c.start()

    slot = b % GATHER_SLOTS
    for h in range(N_HEADS):
        for g in range(n_gather):
            pltpu.make_async_copy(v_hbm.at[0, h], vbuf.at[slot, h, g], sem.at[slot]).wait()

    lane = lax.broadcasted_iota(jnp.int32, (1, o_ref.shape[1]), 1)
    head = lax.broadcasted_iota(jnp.int32, (N_HEADS, 1), 0)
    here = lane == b
    s_own = jnp.sum(jnp.where(here, own_s[...], 0.0), axis=1, keepdims=True)
    tiles = []
    for g in range(n_gather):
        t = jnp.zeros((N_HEADS, wa_ref.shape[3]), _F32)
        for h in range(N_HEADS):
            pg = pl.ds(chosen_page(b, h, g), 1)
            t = jnp.where(head == h, jnp.where(b < split, wa_ref[0, pg][0], wb_ref[0, pg][0]), t)
        tiles.append(t * SCALE)
    m = s_own
    for t in tiles:
        m = jnp.maximum(m, jnp.max(t, axis=1, keepdims=True))
    p_own = jnp.exp(s_own - m)
    denom = p_own
    probs = []
    for t in tiles:
        p = jnp.exp(t - m)
        denom = denom + jnp.sum(p, axis=1, keepdims=True)
        probs.append(p)
    inv = 1.0 / denom
    for h in range(N_HEADS):
        acc = jnp.zeros((HEAD_DIM, probs[0].shape[1]), _F32)
        for g in range(n_gather):
            acc = acc + probs[g][h:h + 1, :] * vbuf[slot, h, g]
        hs = slice(h * HEAD_DIM, (h + 1) * HEAD_DIM)
        o_ref[hs, :] = jnp.where(here, jnp.sum(acc, axis=1, keepdims=True) * inv[h:h + 1, :], o_ref[hs, :])
    coef_s[...] = jnp.where(here, p_own * inv, coef_s[...])

    @pl.when(b == last)
    def _():
        for h in range(N_HEADS):
            hs = slice(h * HEAD_DIM, (h + 1) * HEAD_DIM)
            o_ref[hs, :] = o_ref[hs, :] + coef_s[h:h + 1, :] * vt_ref[hs, :]


def _decode_dims(page_table, cache_kt):
    db, n_pages = page_table.shape
    page = cache_kt.shape[3]
    assert MOBA_BLOCK % page == 0 and (n_pages * page) % MOBA_BLOCK == 0
    n_sel = min(MOBA_TOP_K, n_pages * page // MOBA_BLOCK)
    assert n_sel > 0
    return db, n_pages, page, n_sel


def _decode_scores(qt, cache_kt, page_table):
    db, n_pages, page, n_sel = _decode_dims(page_table, cache_kt)
    assert n_pages % DEC_CHUNK == 0 and db * (n_pages // DEC_CHUNK) >= DEC_SLOTS
    return pl.pallas_call(
        functools.partial(_dec_scores_kernel, n_pages=n_pages, n_sel=n_sel),
        grid_spec=pltpu.PrefetchScalarGridSpec(
            num_scalar_prefetch=1,
            grid=(db,),
            in_specs=[pl.BlockSpec(qt.shape, lambda i, *_: (0, 0)), pl.BlockSpec(memory_space=pl.ANY)],
            out_specs=[
                pl.BlockSpec((1, n_pages, N_HEADS, page), lambda i, *_: (i, 0, 0, 0)),
                pl.BlockSpec((1, n_sel, N_HEADS, LANES), lambda i, *_: (i, 0, 0, 0)),
            ],
            scratch_shapes=[
                pltpu.VMEM((DEC_SLOTS, DEC_CHUNK, N_HEADS, HEAD_DIM, page), _F32),
                pltpu.VMEM((N_HEADS, HEAD_DIM, LANES), _F32),
                pltpu.SemaphoreType.DMA((DEC_SLOTS,)),
            ],
        ),
        out_shape=[
            jax.ShapeDtypeStruct((db, n_pages, N_HEADS, page), _F32),
            jax.ShapeDtypeStruct((db, n_sel, N_HEADS, LANES), jnp.int32),
        ],
        compiler_params=_params("arbitrary"),
        name="decode_scores",
    )(page_table.reshape(-1), qt, cache_kt)


def _decode_attend(logits_a, logits_b, sel, qt, kt, vt, cache_vt, page_table):
    db, n_pages, page, n_sel = _decode_dims(page_table, cache_vt)
    split = logits_a.shape[0]
    assert db >= GATHER_SLOTS
    whole = lambda a: pl.BlockSpec(a.shape, lambda i, *_: (0,) * a.ndim)
    n_gather = n_sel * (MOBA_BLOCK // page)
    out_t = pl.pallas_call(
        functools.partial(_dec_attend_kernel, n_pages=n_pages, n_sel=n_sel, split=split),
        grid_spec=pltpu.PrefetchScalarGridSpec(
            num_scalar_prefetch=2,
            grid=(db,),
            in_specs=[
                pl.BlockSpec((1, n_pages, N_HEADS, page), lambda i, *_: (jnp.minimum(i, split - 1), 0, 0, 0)),
                pl.BlockSpec((1, n_pages, N_HEADS, page), lambda i, *_: (jnp.maximum(i - split, 0), 0, 0, 0)),
                whole(qt), whole(kt), whole(vt), pl.BlockSpec(memory_space=pl.ANY),
            ],
            out_specs=pl.BlockSpec((ATTN_WIDTH, db), lambda i, *_: (0, 0)),
            scratch_shapes=[
                pltpu.VMEM((GATHER_SLOTS, N_HEADS, n_gather, HEAD_DIM, page), _F32),
                pltpu.VMEM((N_HEADS, db), _F32),
                pltpu.VMEM((N_HEADS, db), _F32),
                pltpu.SemaphoreType.DMA((GATHER_SLOTS,)),
            ],
        ),
        out_shape=jax.ShapeDtypeStruct((ATTN_WIDTH, db), _F32),
        compiler_params=_params("arbitrary"),
        name="decode_attend",
    )(page_table.reshape(-1), sel[:, :, :, 0].reshape(-1), logits_a, logits_b, qt, kt, vt, cache_vt)
    return out_t.T


def _pool_branch(d_groups, wpg_ref, sp_ref):
    outs = []
    for g, d in enumerate(d_groups):
        cols = slice(g * POOL_GROUP_WIDTH, (g + 1) * POOL_GROUP_WIDTH)
        outs.append(_dot(d.astype(_MXU), wpg_ref[g]) * sp_ref[:, cols])
    return jnp.concatenate(outs, axis=1)


def _mix_tail(x, pool, attn_proj, gt_ref, wbp_ref, wo_ref, o_ref):
    d = x.shape[1]
    m = gt_ref[:, :d] * _dot(pool.astype(_MXU), wbp_ref[...]) + gt_ref[:, d:] * attn_proj
    o_ref[...] = x + _dot(m.astype(_MXU), wo_ref[...])


def _mix_prompt_kernel(x_ref, at_ref, u_ref, hist_ref, gt_ref, wpg_ref, sp_ref, wbp_ref, wba_ref,
                       wo_ref, o_ref, ext, lvl, *, seq):
    tm = u_ref.shape[0]
    pad = 8
    n = tm + HIST_ROWS
    pos0 = (pl.program_id(0) * tm) % seq
    ext[0:pad, :] = jnp.zeros((pad, POOL_WIDTH), _F32)
    ext[pad:pad + HIST_ROWS, :] = jnp.where(pos0 > 0, hist_ref[...], 0.0)
    ext[pad + HIST_ROWS:, :] = u_ref[...]
    lvl[:, 0:pad, :] = jnp.zeros((2, pad, POOL_GROUP_WIDTH), _F32)
    attn_proj = _dot(at_ref[...].astype(_MXU), wba_ref[...])
    bits = pltpu.bitcast(attn_proj[tm - pad:, attn_proj.shape[1] - POOL_GROUP_WIDTH:], jnp.uint32)
    tie_cols = slice(POOL_WIDTH - POOL_GROUP_WIDTH, POOL_WIDTH)
    ext[0:pad, tie_cols] = ((bits >> 16) >> 16).astype(_F32)
    pos = pos0 + lax.broadcasted_iota(jnp.int32, (tm, 1), 0)
    d_groups = []
    for g, w in enumerate(POOL_WINDOWS):
        assert w & (w - 1) == 0 and w <= HIST_ROWS
        cols = slice(g * POOL_GROUP_WIDTH, (g + 1) * POOL_GROUP_WIDTH)
        read = lambda start: ext[pl.ds(start, n), cols]
        shift = 1
        while True:
            total = read(pad) + read(pad - shift)
            shift *= 2
            if shift == w:
                break
            slot = lvl.at[(shift.bit_length()) % 2]
            slot[pl.ds(pad, n), :] = total
            read = lambda start, slot=slot: slot[pl.ds(start, n), :]
        cur = ext[pad + HIST_ROWS:, cols]
        cnt = jnp.minimum(w, pos + 1).astype(_F32)
        d_groups.append(total[HIST_ROWS:] / cnt - cur)
    pool = _pool_branch(d_groups, wpg_ref, sp_ref)
    _mix_tail(x_ref[...], pool, attn_proj, gt_ref, wbp_ref, wo_ref, o_ref)


def _mix_decode_kernel(x_ref, at_ref, u_ref, hist_ref, gt_ref, wpg_ref, sp_ref, wbp_ref, wba_ref,
                       wo_ref, o_ref, *, past):
    d_groups = []
    for g, w in enumerate(POOL_WINDOWS):
        cols = slice(g * POOL_GROUP_WIDTH, (g + 1) * POOL_GROUP_WIDTH)
        cur = u_ref[:, cols]
        total = cur
        for j in range(1, w):
            total = total + hist_ref[POOL_HIST - j, :, cols]
        d_groups.append(total / float(min(w, past + 1)) - cur)
    pool = _pool_branch(d_groups, wpg_ref, sp_ref)
    attn_proj = _dot(at_ref[...].astype(_MXU), wba_ref[...])
    _mix_tail(x_ref[...], pool, attn_proj, gt_ref, wbp_ref, wo_ref, o_ref)


def _mix(x, attn, u, hist, gates, wpg, sp, wbp, wba, wo, *, seq=None, past=None):
    n, d = x.shape
    tm = _token_tile(n)
    row = lambda w: pl.BlockSpec((tm, w), lambda i: (i, 0))
    if seq is not None:
        assert seq % tm == 0 and tm % HIST_ROWS == 0
        body = functools.partial(_mix_prompt_kernel, seq=seq)
        hist_spec = pl.BlockSpec(
            (HIST_ROWS, POOL_WIDTH), lambda i: (jnp.maximum(i * (tm // HIST_ROWS) - 1, 0), 0))
        scratch = [pltpu.VMEM((8 + HIST_ROWS + tm, POOL_WIDTH), _F32),
                   pltpu.VMEM((2, 8 + HIST_ROWS + tm, POOL_GROUP_WIDTH), _F32)]
    else:
        body = functools.partial(_mix_decode_kernel, past=past)
        hist_spec = pl.BlockSpec((POOL_HIST, tm, POOL_WIDTH), lambda i: (0, i, 0))
        scratch = []
    return pl.pallas_call(
        body,
        grid=(n // tm,),
        in_specs=[
            row(d), row(ATTN_WIDTH), row(POOL_WIDTH), hist_spec, row(2 * d),
            _resident(wpg.shape), _resident(sp.shape), _resident(wbp.shape),
            _resident(wba.shape), _resident(wo.shape),
        ],
        out_specs=row(d),
        out_shape=jax.ShapeDtypeStruct((n, d), _F32),
        scratch_shapes=scratch,
        compiler_params=_params("parallel"),
        name="mix",
    )(x, attn, u, hist, gates, wpg, sp, wbp, wba, wo)


def kernel(x_prompt, x_sample, cache_k, cache_v, state_pool, page_table, g_ffn1, w1_ffn1, w3_ffn1,
           w2_ffn1, g_mix, w_in, b_gate, g_q, g_k, w_pool_grp, s_pool, w_branch_pool,
           w_branch_attn, w_out, g_ffn2, w1_ffn2, w3_ffn2, w2_ffn2):
    assert w_in.shape[0] == 1 and x_sample.shape[1] == 1
    b, s, d = x_prompt.shape
    db = x_sample.shape[0]
    page = cache_k.shape[2]
    past = page_table.shape[1] * page
    mx = lambda w: w[0].astype(_MXU)

    ffn1 = (g_ffn1, mx(w1_ffn1), mx(w3_ffn1), mx(w2_ffn1))
    ffn2 = (g_ffn2, mx(w1_ffn2), mx(w3_ffn2), mx(w2_ffn2))
    head_gain = lambda g: jnp.tile(g, (1, N_HEADS)).reshape(ATTN_WIDTH, 1)
    proj = (g_mix, mx(w_in), b_gate, head_gain(g_q), head_gain(g_k))
    post = (mx(w_pool_grp), s_pool, mx(w_branch_pool), mx(w_branch_attn), mx(w_out))

    xs = _ffn(x_sample.reshape(db, d), *ffn1)
    us, qs, _, _, gates_s, kst, vst = _proj(xs, *proj, seq=db)
    qst = qs.T
    cache_kt = jnp.transpose(cache_k[0], (0, 2, 3, 1))
    cache_vt = jnp.transpose(cache_v[0], (0, 2, 3, 1))

    steps = (b * s) // _token_tile(b * s)
    ride = db == 2 * steps
    if ride:
        _, n_pages, _, n_sel = _decode_dims(page_table, cache_kt)
        rider = lambda base: dict(pt=page_table.reshape(-1), qt=qst, cache_kt=cache_kt, base=base,
                                  n_pages=n_pages, n_sel=n_sel)
        xp, logits_a, sel_a = _ffn_scores(x_prompt.reshape(b * s, d), *ffn1, **rider(0))
    else:
        xp = _ffn(x_prompt.reshape(b * s, d), *ffn1)
    u, q, k, v, gates, kt, vt = _proj(xp, *proj, seq=s)
    shape3 = (b, s, ATTN_WIDTH)
    attn = _moba_prompt(q.reshape(shape3), k.reshape(shape3), v.reshape(shape3))
    xp = _mix(xp, attn.reshape(b * s, ATTN_WIDTH), u, u, gates, *post, seq=s)
    if ride:
        yp, logits_b, sel_b = _ffn_scores(xp, *ffn2, **rider(steps))
        sel = jnp.concatenate([sel_a, sel_b], axis=0)
    else:
        yp = _ffn(xp, *ffn2)
        logits_a, sel = _decode_scores(qst, cache_kt, page_table)
        logits_b = logits_a
    yp = yp.reshape(b, s, d)

    attn_s = _decode_attend(logits_a, logits_b, sel, qst, kst[0], vst[0], cache_vt, page_table)
    hist = jnp.transpose(state_pool[0], (1, 0, 2))
    xs = _mix(xs, attn_s, us, hist, gates_s, *post, past=past)
    ys = _ffn(xs, *ffn2).reshape(db, 1, d)

    cache_p = lambda t: jnp.transpose(t.reshape(1, b, N_HEADS, HEAD_DIM, s), (0, 1, 4, 2, 3))
    cache_s = lambda t: jnp.transpose(t.reshape(1, 1, N_HEADS, HEAD_DIM, db), (0, 4, 1, 2, 3))
    pool_p = u.reshape(b, s, POOL_WIDTH)[:, s - POOL_HIST:]
    pool_s = jnp.concatenate([state_pool[0, :, 1:], us[:, None]], axis=1)
    return (yp, ys, cache_p(kt), cache_p(vt), pool_p[None], cache_s(kst), cache_s(vst), pool_s[None])
```
